```python
import math
import jax, jax.numpy as jnp
from jax import lax
import numpy as np

D_MODEL = 1024
BATCH = 4
SEQ = 4096
DEPTH = 4
DEC_BATCH = 32
DEC_SEQ = 8
PAST_LEN = 8192
PAGE_SIZE = 128

HEAD_DIM = 64
SSM_WIDTH = D_MODEL // 4
SSM_CH = 16
SSM_GROUPS = SSM_WIDTH // SSM_CH
SSM_STATE = 64
SB_HEADS = (D_MODEL - SSM_WIDTH) // (2 * HEAD_DIM)
SB_WIDTH = SB_HEADS * HEAD_DIM
GDN_HEADS = (D_MODEL - SSM_WIDTH - SB_WIDTH) // HEAD_DIM
GDN_WIDTH = GDN_HEADS * HEAD_DIM
CONV_WIDTH = 4
GDN_CONV_CH = 3 * GDN_WIDTH
GDN_CHUNK = 64
SB_BLOCK = 128
D_FF = ((8 * D_MODEL + 3 * 256 - 1) // (3 * 256)) * 256
N_ADA = 6
IN_SIZES = (SB_WIDTH, SB_WIDTH, SB_WIDTH, SSM_WIDTH, GDN_WIDTH, GDN_WIDTH, GDN_WIDTH, GDN_WIDTH, GDN_HEADS, GDN_HEADS)
IN_WIDTH = sum(IN_SIZES)
EPS = 1e-6

kernel_name = "hybrid_sb_s5_gdn_adaln_step"


def rms_norm(x, gain):
    xf = x.astype(jnp.float32)
    y = xf * lax.rsqrt(jnp.mean(xf * xf, axis=-1, keepdims=True) + EPS)
    return (y * gain.astype(jnp.float32)).astype(x.dtype)


def l2_norm(x):
    xf = x.astype(jnp.float32)
    return xf * lax.rsqrt(jnp.sum(xf * xf, axis=-1, keepdims=True) + EPS)


def stick_breaking_attention(q, k, v, q_pos, k_pos, bias):
    B, Sq, H, Dh = q.shape
    qb = math.gcd(Sq, SB_BLOCK)
    nb = Sq // qb
    q_blocks = jnp.swapaxes(q.reshape(B, nb, qb, H, Dh), 0, 1)
    p_blocks = q_pos.reshape(nb, qb)
    inv_sqrt_d = 1.0 / math.sqrt(Dh)
    bias_f = bias.astype(jnp.float32)[None, :, None, None]

    def one_block(args):
        qblk, pblk = args
        z = jnp.einsum('bqhd,bkhd->bhqk', qblk, k).astype(jnp.float32) * inv_sqrt_d + bias_f
        causal = (k_pos[None, :] < pblk[:, None])[None, None]
        log_beta = jax.nn.log_sigmoid(z)
        log_keep = jnp.where(causal, jax.nn.log_sigmoid(-z), 0.0)
        after = lax.cumsum(log_keep, axis=3, reverse=True) - log_keep
        w = jnp.where(causal, jnp.exp(log_beta + after), 0.0)
        return jnp.einsum('bhqk,bkhd->bqhd', w.astype(v.dtype), v)

    out = lax.map(one_block, (q_blocks, p_blocks))
    return jnp.swapaxes(out, 0, 1).reshape(B, Sq, H, Dh)


def s5_mixer(u, x0_re, x0_im, a_re, a_im, log_dt, b_re, b_im, c_re, c_im, d_skip, w_glu):
    B, S, _ = u.shape
    f32 = jnp.float32
    uf = u.astype(f32).reshape(B, S, SSM_GROUPS, SSM_CH)
    dt = jnp.exp(log_dt.astype(f32))[:, None]
    lam_re = jnp.minimum(a_re.astype(f32), -1e-4)
    lam_im = a_im.astype(f32)
    mag = jnp.exp(lam_re * dt)
    ab_re = mag * jnp.cos(lam_im * dt)
    ab_im = mag * jnp.sin(lam_im * dt)
    den = lam_re * lam_re + lam_im * lam_im
    nr = ab_re - 1.0
    f_re = (nr * lam_re + ab_im * lam_im) / den
    f_im = (ab_im * lam_re - nr * lam_im) / den
    br = b_re.astype(f32)
    bi = b_im.astype(f32)
    bb_re = f_re[..., None] * br - f_im[..., None] * bi
    bb_im = f_re[..., None] * bi + f_im[..., None] * br
    bu_re = jnp.einsum('bsgc,gpc->bsgp', uf, bb_re)
    bu_im = jnp.einsum('bsgc,gpc->bsgp', uf, bb_im)
    x0r = x0_re.astype(f32)
    x0i = x0_im.astype(f32)
    bu_re = bu_re.at[:, 0].add(ab_re * x0r - ab_im * x0i)
    bu_im = bu_im.at[:, 0].add(ab_re * x0i + ab_im * x0r)
    a_re_s = jnp.broadcast_to(ab_re, bu_re.shape)
    a_im_s = jnp.broadcast_to(ab_im, bu_im.shape)

    def combine(e1, e2):
        a1r, a1i, b1r, b1i = e1
        a2r, a2i, b2r, b2i = e2
        return (a2r * a1r - a2i * a1i, a2r * a1i + a2i * a1r,
                a2r * b1r - a2i * b1i + b2r, a2r * b1i + a2i * b1r + b2i)

    _, _, xr, xi = lax.associative_scan(combine, (a_re_s, a_im_s, bu_re, bu_im), axis=1)
    y = (jnp.einsum('bsgp,gcp->bsgc', xr, c_re.astype(f32))
         - jnp.einsum('bsgp,gcp->bsgc', xi, c_im.astype(f32))
         + d_skip.astype(f32).reshape(SSM_GROUPS, SSM_CH) * uf)
    g = jax.nn.gelu(y.reshape(B, S, SSM_WIDTH))
    out = g * jax.nn.sigmoid(g @ w_glu.astype(f32))
    return out.astype(u.dtype), xr[:, -1], xi[:, -1]


def gated_delta_rule(q, k, v, g, beta, s0):
    B, S, H, DK = q.shape
    f32 = jnp.float32
    C = math.gcd(S, GDN_CHUNK)
    N = S // C

    def to_chunks(t):
        return jnp.swapaxes(t.astype(f32).reshape((B, N, C) + t.shape[2:]), 2, 3)

    qc = to_chunks(q) * (DK ** -0.5)
    kc = to_chunks(k)
    vc = to_chunks(v)
    gc = to_chunks(g)
    bc = to_chunks(beta)
    G = jnp.cumsum(gc, axis=-1)
    idx = jnp.arange(C)
    lower = idx[:, None] >= idx[None, :]
    strict = idx[:, None] > idx[None, :]
    decay = jnp.exp(jnp.where(lower, G[..., :, None] - G[..., None, :], -jnp.inf))
    kb = kc * bc[..., None]
    vb = vc * bc[..., None]
    m = jnp.where(strict, jnp.einsum('bnhid,bnhjd->bnhij', kb, kc) * decay, 0.0)
    eye = jnp.eye(C, dtype=f32)
    t_inv = lax.linalg.triangular_solve(eye + m, jnp.broadcast_to(eye, m.shape), left_side=True, lower=True)
    u_c = t_inv @ vb
    w_c = t_inv @ (kb * jnp.exp(G)[..., None])
    attn = jnp.einsum('bnhid,bnhjd->bnhij', qc, kc) * decay
    q_dec = qc * jnp.exp(G)[..., None]
    k_dec = kc * jnp.exp(G[..., -1:] - G)[..., None]
    g_last = jnp.exp(G[..., -1])

    def step(state, xs):
        u_n, w_n, a_n, qd_n, kd_n, gl_n = xs
        v_new = u_n - w_n @ state
        o = qd_n @ state + a_n @ v_new
        state = state * gl_n[..., None, None] + jnp.swapaxes(kd_n, -1, -2) @ v_new
        return state, o

    xs = tuple(jnp.moveaxis(t, 1, 0) for t in (u_c, w_c, attn, q_dec, k_dec, g_last))
    s_fin, o = lax.scan(step, s0.astype(f32), xs)
    o = jnp.transpose(o, (1, 0, 3, 2, 4)).reshape(B, S, H, v.shape[-1])
    return o, s_fin


def token_mixers(h, lp, k_past, v_past, conv_buf, ssm_re0, ssm_im0, gdn_s0):
    B, S, _ = h.shape
    proj = h @ lp["w_in"]
    qa, ka, va, u, qc, kc, vc, z, a_in, b_in = jnp.split(proj, np.cumsum(IN_SIZES)[:-1].tolist(), axis=-1)

    qa = rms_norm(qa.reshape(B, S, SB_HEADS, HEAD_DIM), lp["sb_qnorm"])
    ka = rms_norm(ka.reshape(B, S, SB_HEADS, HEAD_DIM), lp["sb_knorm"])
    va = va.reshape(B, S, SB_HEADS, HEAD_DIM)
    if k_past is None:
        past = 0
        k_all, v_all = ka, va
    else:
        past = k_past.shape[1]
        k_all = jnp.concatenate([k_past.astype(ka.dtype), ka], axis=1)
        v_all = jnp.concatenate([v_past.astype(va.dtype), va], axis=1)
    q_pos = past + jnp.arange(S)
    k_pos = jnp.arange(past + S)
    oa = stick_breaking_attention(qa, k_all, v_all, q_pos, k_pos, lp["sb_bias"]).reshape(B, S, SB_WIDTH)

    ob, ssm_re, ssm_im = s5_mixer(u, ssm_re0, ssm_im0, lp["ssm_a_re"], lp["ssm_a_im"], lp["ssm_log_dt"],
                                  lp["ssm_b_re"], lp["ssm_b_im"], lp["ssm_c_re"], lp["ssm_c_im"],
                                  lp["ssm_d"], lp["ssm_w_glu"])

    xc = jnp.concatenate([qc, kc, vc], axis=-1)
    ext = jnp.concatenate([conv_buf.astype(xc.dtype), xc], axis=1)
    w_conv = lp["gdn_conv"]
    conv = ext[:, 0:S] * w_conv[0]
    for i in range(1, CONV_WIDTH):
        conv = conv + ext[:, i:i + S] * w_conv[i]
    conv = jax.nn.silu(conv)
    new_buf = ext[:, S:]
    qg, kg, vg = jnp.split(conv, 3, axis=-1)
    qg = l2_norm(qg.reshape(B, S, GDN_HEADS, HEAD_DIM))
    kg = l2_norm(kg.reshape(B, S, GDN_HEADS, HEAD_DIM))
    vg = vg.reshape(B, S, GDN_HEADS, HEAD_DIM)
    g = -jnp.exp(lp["gdn_a_log"].astype(jnp.float32)) * jax.nn.softplus(
        a_in.astype(jnp.float32) + lp["gdn_dt_bias"].astype(jnp.float32))
    beta = jax.nn.sigmoid(b_in.astype(jnp.float32))
    o, gdn_s = gated_delta_rule(qg, kg, vg, g, beta, gdn_s0)
    oc = rms_norm(o.astype(h.dtype), lp["gdn_norm"]) * jax.nn.silu(z.reshape(B, S, GDN_HEADS, HEAD_DIM))
    oc = oc.reshape(B, S, GDN_WIDTH)

    out = jnp.concatenate([oa.astype(h.dtype), ob, oc.astype(h.dtype)], axis=-1) @ lp["w_out"]
    return out, (ka, va, new_buf, ssm_re, ssm_im, gdn_s)


def decoder_layer(x, c, lp, k_past, v_past, conv_buf, ssm_re0, ssm_im0, gdn_s0):
    mod = jax.nn.silu(c) @ lp["w_ada"] + lp["b_ada"]
    sh1, sc1, g1, sh2, sc2, g2 = [m[:, None, :] for m in jnp.split(mod, N_ADA, axis=-1)]
    h = rms_norm(x, lp["norm_mix"]) * (1 + sc1) + sh1
    mix, new_state = token_mixers(h, lp, k_past, v_past, conv_buf, ssm_re0, ssm_im0, gdn_s0)
    x = x + g1 * mix
    h = rms_norm(x, lp["norm_ffn"]) * (1 + sc2) + sh2
    ff = (jax.nn.silu(h @ lp["ffn_gate"]) * (h @ lp["ffn_up"])) @ lp["ffn_down"]
    x = x + g2 * ff
    return x, new_state


def setup_inputs(seed: int = 0) -> dict:
    key = jax.random.key(seed)
    ks = iter(jax.random.split(key, 48))
    f32 = jnp.float32
    n_pages = PAST_LEN // PAGE_SIZE
    n_used = DEC_BATCH * n_pages
    n_pool = (5 * n_used + 3) // 4

    def nrm(shape, scale):
        return jax.random.normal(next(ks), shape, f32) * scale

    def unif(shape, lo, hi):
        return jax.random.uniform(next(ks), shape, f32, lo, hi)

    x_prompt = nrm((BATCH, SEQ, D_MODEL), 1.0)
    x_sample = nrm((DEC_BATCH, DEC_SEQ, D_MODEL), 1.0)
    cache_k = nrm((DEPTH, n_pool, PAGE_SIZE, SB_HEADS, HEAD_DIM), 1.0)
    cache_v = nrm((DEPTH, n_pool, PAGE_SIZE, SB_HEADS, HEAD_DIM), 1.0)
    state_conv = nrm((DEPTH, DEC_BATCH, CONV_WIDTH - 1, GDN_CONV_CH), 1.0)
    state_ssm_re = nrm((DEPTH, DEC_BATCH, SSM_GROUPS, SSM_STATE), 0.5)
    state_ssm_im = nrm((DEPTH, DEC_BATCH, SSM_GROUPS, SSM_STATE), 0.5)
    state_gdn = nrm((DEPTH, DEC_BATCH, GDN_HEADS, HEAD_DIM, HEAD_DIM), 0.1)
    page_table = jax.random.permutation(next(ks), n_pool)[:n_used].reshape(DEC_BATCH, n_pages).astype(jnp.int32)
    c_prompt = nrm((BATCH, D_MODEL), 1.0)
    c_sample = nrm((DEC_BATCH, D_MODEL), 1.0)

    w_ada = nrm((DEPTH, D_MODEL, N_ADA * D_MODEL), 0.5 * D_MODEL ** -0.5)
    b_ada = nrm((DEPTH, N_ADA * D_MODEL), 0.02)
    norm_mix = 1.0 + nrm((DEPTH, D_MODEL), 0.02)
    norm_ffn = 1.0 + nrm((DEPTH, D_MODEL), 0.02)
    w_in = nrm((DEPTH, D_MODEL, IN_WIDTH), D_MODEL ** -0.5)
    sb_qnorm = 1.0 + nrm((DEPTH, HEAD_DIM), 0.02)
    sb_knorm = 1.0 + nrm((DEPTH, HEAD_DIM), 0.02)
    sb_bias = unif((DEPTH, SB_HEADS), -8.0, -5.0)
    ssm_a_re = -0.5 + nrm((DEPTH, SSM_GROUPS, SSM_STATE), 0.01)
    ssm_a_im = math.pi * jnp.arange(SSM_STATE, dtype=f32) + nrm((DEPTH, SSM_GROUPS, SSM_STATE), 0.01)
    ssm_log_dt = unif((DEPTH, SSM_GROUPS), math.log(1e-3), math.log(1e-1))
    ssm_b_re = nrm((DEPTH, SSM_GROUPS, SSM_STATE, SSM_CH), (2 * SSM_CH) ** -0.5)
    ssm_b_im = nrm((DEPTH, SSM_GROUPS, SSM_STATE, SSM_CH), (2 * SSM_CH) ** -0.5)
    ssm_c_re = nrm((DEPTH, SSM_GROUPS, SSM_CH, SSM_STATE), (2 * SSM_STATE) ** -0.5)
    ssm_c_im = nrm((DEPTH, SSM_GROUPS, SSM_CH, SSM_STATE), (2 * SSM_STATE) ** -0.5)
    ssm_d = nrm((DEPTH, SSM_WIDTH), 1.0)
    ssm_w_glu = nrm((DEPTH, SSM_WIDTH, SSM_WIDTH), SSM_WIDTH ** -0.5)
    gdn_conv = nrm((DEPTH, CONV_WIDTH, GDN_CONV_CH), CONV_WIDTH ** -0.5)
    gdn_a_log = jnp.log(unif((DEPTH, GDN_HEADS), 1.0, 16.0))
    dt0 = jnp.exp(unif((DEPTH, GDN_HEADS), math.log(1e-3), math.log(1e-1)))
    gdn_dt_bias = dt0 + jnp.log(-jnp.expm1(-dt0))
    gdn_norm = 1.0 + nrm((DEPTH, HEAD_DIM), 0.02)
    w_out = nrm((DEPTH, D_MODEL, D_MODEL), D_MODEL ** -0.5)
    ffn_gate = nrm((DEPTH, D_MODEL, D_FF), D_MODEL ** -0.5)
    ffn_up = nrm((DEPTH, D_MODEL, D_FF), D_MODEL ** -0.5)
    ffn_down = nrm((DEPTH, D_FF, D_MODEL), D_FF ** -0.5)
    return {
        "x_prompt": x_prompt, "x_sample": x_sample,
        "cache_k": cache_k, "cache_v": cache_v, "state_conv": state_conv,
        "state_ssm_re": state_ssm_re, "state_ssm_im": state_ssm_im, "state_gdn": state_gdn,
        "page_table": page_table, "c_prompt": c_prompt, "c_sample": c_sample,
        "w_ada": w_ada, "b_ada": b_ada, "norm_mix": norm_mix, "norm_ffn": norm_ffn, "w_in": w_in,
        "sb_qnorm": sb_qnorm, "sb_knorm": sb_knorm, "sb_bias": sb_bias,
        "ssm_a_re": ssm_a_re, "ssm_a_im": ssm_a_im, "ssm_log_dt": ssm_log_dt,
        "ssm_b_re": ssm_b_re, "ssm_b_im": ssm_b_im, "ssm_c_re": ssm_c_re, "ssm_c_im": ssm_c_im,
        "ssm_d": ssm_d, "ssm_w_glu": ssm_w_glu,
        "gdn_conv": gdn_conv, "gdn_a_log": gdn_a_log, "gdn_dt_bias": gdn_dt_bias, "gdn_norm": gdn_norm,
        "w_out": w_out, "ffn_gate": ffn_gate, "ffn_up": ffn_up, "ffn_down": ffn_down,
    }


def reference(x_prompt, x_sample, cache_k, cache_v, state_conv, state_ssm_re, state_ssm_im, state_gdn,
              page_table, c_prompt, c_sample, w_ada, b_ada, norm_mix, norm_ffn, w_in, sb_qnorm, sb_knorm,
              sb_bias, ssm_a_re, ssm_a_im, ssm_log_dt, ssm_b_re, ssm_b_im, ssm_c_re, ssm_c_im, ssm_d, ssm_w_glu,
              gdn_conv, gdn_a_log, gdn_dt_bias, gdn_norm, w_out, ffn_gate, ffn_up, ffn_down):
    bp = x_prompt.shape[0]
    bs = x_sample.shape[0]
    past_len = page_table.shape[1] * cache_k.shape[2]
    xp, xs = x_prompt, x_sample
    st_p = [[] for _ in range(6)]
    st_s = [[] for _ in range(6)]
    for l in range(DEPTH):
        lp = {
            "w_ada": w_ada[l], "b_ada": b_ada[l], "norm_mix": norm_mix[l], "norm_ffn": norm_ffn[l],
            "w_in": w_in[l], "sb_qnorm": sb_qnorm[l], "sb_knorm": sb_knorm[l], "sb_bias": sb_bias[l],
            "ssm_a_re": ssm_a_re[l], "ssm_a_im": ssm_a_im[l], "ssm_log_dt": ssm_log_dt[l],
            "ssm_b_re": ssm_b_re[l], "ssm_b_im": ssm_b_im[l], "ssm_c_re": ssm_c_re[l], "ssm_c_im": ssm_c_im[l],
            "ssm_d": ssm_d[l], "ssm_w_glu": ssm_w_glu[l],
            "gdn_conv": gdn_conv[l], "gdn_a_log": gdn_a_log[l], "gdn_dt_bias": gdn_dt_bias[l],
            "gdn_norm": gdn_norm[l], "w_out": w_out[l],
            "ffn_gate": ffn_gate[l], "ffn_up": ffn_up[l], "ffn_down": ffn_down[l],
        }
        xp, new_p = decoder_layer(
            xp, c_prompt, lp, None, None,
            jnp.zeros((bp, CONV_WIDTH - 1, GDN_CONV_CH), xp.dtype),
            jnp.zeros((bp, SSM_GROUPS, SSM_STATE), jnp.float32),
            jnp.zeros((bp, SSM_GROUPS, SSM_STATE), jnp.float32),
            jnp.zeros((bp, GDN_HEADS, HEAD_DIM, HEAD_DIM), jnp.float32))
        k_past = cache_k[l][page_table].reshape(bs, past_len, SB_HEADS, HEAD_DIM)
        v_past = cache_v[l][page_table].reshape(bs, past_len, SB_HEADS, HEAD_DIM)
        xs, new_s = decoder_layer(xs, c_sample, lp, k_past, v_past, state_conv[l],
                                  state_ssm_re[l], state_ssm_im[l], state_gdn[l])
        for i in range(6):
            st_p[i].append(new_p[i])
            st_s[i].append(new_s[i])
    k_p, v_p, conv_p, ssm_re_p, ssm_im_p, gdn_p = [jnp.stack(t) for t in st_p]
    k_s, v_s, conv_s, ssm_re_s, ssm_im_s, gdn_s = [jnp.stack(t) for t in st_s]
    return (xp, xs, k_p, v_p, conv_p, ssm_re_p, ssm_im_p, gdn_p, k_s, v_s, conv_s, ssm_re_s, ssm_im_s, gdn_s)
```

```python
import functools
import math

import jax
import jax.numpy as jnp
from jax import lax
from jax.experimental import pallas as pl
from jax.experimental.pallas import tpu as pltpu

F32 = jnp.float32
BF16 = jnp.bfloat16
EPS = 1e-6
HEAD_DIM = 64
SSM_CH = 16
SSM_STATE = 64
CONV_WIDTH = 4
GDN_CHUNK = 64
N_ADA = 6
LANES = 128
SUBLANES = 8
VMEM_LIMIT = 56 * 1024 * 1024


def _cparams(*sem):
    return pltpu.CompilerParams(dimension_semantics=sem, vmem_limit_bytes=VMEM_LIMIT)


def _bdot(a, b):
    return jnp.dot(a.astype(BF16), b.astype(BF16), preferred_element_type=F32)


def _split(x):
    hi = x.astype(BF16)
    lo = (x - hi.astype(F32)).astype(BF16)
    return hi, lo


def _dot2(x, m):
    hi, lo = _split(x)
    return jnp.dot(hi, m, preferred_element_type=F32) + jnp.dot(lo, m, preferred_element_type=F32)


def _dot2_left(m, x):
    hi, lo = _split(x)
    return jnp.dot(m, hi, preferred_element_type=F32) + jnp.dot(m, lo, preferred_element_type=F32)


def _sigmoid(x):
    return 1.0 / (1.0 + jnp.exp(-x))


def _softplus(x):
    return jnp.maximum(x, 0.0) + jnp.log(1.0 + jnp.exp(-jnp.abs(x)))


def _full(shape):
    n = len(shape)
    return pl.BlockSpec(shape, lambda *_: (0,) * n)


def _ada_kernel(c_ref, w_ref, b_ref, o_ref):
    c = c_ref[...]
    s = c * _sigmoid(c)
    o_ref[...] = _bdot(s, w_ref[...]) + b_ref[...]


def _ada(c_all, w_ada, b_ada):
    depth, d, n = w_ada.shape
    r = c_all.shape[0]
    tn = 1536 if n % 1536 == 0 else n
    return pl.pallas_call(
        _ada_kernel,
        grid=(depth, n // tn),
        in_specs=[
            pl.BlockSpec((r, d), lambda l, j: (0, 0)),
            pl.BlockSpec((None, d, tn), lambda l, j: (l, 0, j)),
            pl.BlockSpec((None, 1, tn), lambda l, j: (l, 0, j)),
        ],
        out_specs=pl.BlockSpec((None, r, tn), lambda l, j: (l, 0, j)),
        out_shape=jax.ShapeDtypeStruct((depth, r, n), F32),
        compiler_params=_cparams("parallel", "parallel"),
        name="ada_mod",
    )(c_all, w_ada, b_ada.reshape(depth, 1, n))


def _inproj_kernel(x_ref, sc_ref, sh_ref, g_ref, w_ref, qg_ref, kg_ref, hm_ref,
                   q_ref, k_ref, v_ref, u_ref, xc_ref, z_ref, ab_ref, *, offs):
    x = x_ref[...]
    h = x * lax.rsqrt(jnp.mean(x * x, axis=-1, keepdims=True) + EPS) * g_ref[...]
    h = h * (1.0 + sc_ref[...]) + sh_ref[...]
    hb = h.astype(BF16)

    def proj(a, b):
        return jnp.dot(hb, w_ref[:, a:b], preferred_element_type=F32)

    hm = hm_ref[...]

    def headnorm(t, gain):
        ms = _dot2(t * t, hm) * (1.0 / HEAD_DIM)
        return t * lax.rsqrt(ms + EPS) * gain

    o_q, o_k, o_v, o_u, o_xc, o_z, o_ab, o_end = offs
    q_ref[...] = (headnorm(proj(o_q, o_k), qg_ref[...]) * (HEAD_DIM ** -0.5)).astype(BF16)
    k_ref[...] = headnorm(proj(o_k, o_v), kg_ref[...])
    v_ref[...] = proj(o_v, o_u)
    u_ref[...] = proj(o_u, o_xc)
    xc_ref[...] = proj(o_xc, o_z)
    z_ref[...] = proj(o_z, o_ab)
    ab_ref[...] = proj(o_ab, o_end)


def _inproj(x, sc, sh, gain, w_bf, qg, kg, hm, offs, tb):
    n, d = x.shape
    nb = n // tb
    per = nb // sc.shape[0]
    r = sc.shape[1]
    widths = [offs[i + 1] - offs[i] for i in range(7)]
    dts = [BF16] + [F32] * 6
    mod_spec = pl.BlockSpec((None, r, d), lambda i: (i // per, 0, 0))
    return pl.pallas_call(
        functools.partial(_inproj_kernel, offs=offs),
        grid=(nb,),
        in_specs=[
            pl.BlockSpec((tb, d), lambda i: (i, 0)),
            mod_spec, mod_spec,
            _full((1, d)),
            _full(w_bf.shape),
            _full(qg.shape), _full(kg.shape), _full(hm.shape),
        ],
        out_specs=[pl.BlockSpec((tb, w), lambda i: (i, 0)) for w in widths],
        out_shape=[jax.ShapeDtypeStruct((n, w), dt) for w, dt in zip(widths, dts)],
        compiler_params=_cparams("parallel"),
        name="inproj",
    )(x, sc, sh, gain, w_bf, qg, kg, hm)


def _sb_tile(z, mask, tri, r_run):
    sp = _softplus(z)
    lk = jnp.where(mask, -sp, 0.0)
    after = _dot2(lk, tri) + r_run
    w = jnp.where(mask, jnp.exp(z - sp + after), 0.0)
    return w, r_run + jnp.sum(lk, axis=1, keepdims=True)


def _attn_kernel(bias_ref, q_ref, kt_ref, v_ref, o_ref, *, tq):
    p = pl.program_id(1)
    i = pl.program_id(2)
    tk = tq
    q2 = q_ref[...]
    lane = lax.broadcasted_iota(jnp.int32, q2.shape, 1)
    row = lax.broadcasted_iota(jnp.int32, (tq, tk), 0)
    col = lax.broadcasted_iota(jnp.int32, (tq, tk), 1)
    tri = (row > col).astype(BF16)
    accs = []
    for hh in range(2):
        qh = jnp.where((lane >= hh * HEAD_DIM) & (lane < (hh + 1) * HEAD_DIM), q2, jnp.zeros_like(q2))
        bias = bias_ref[2 * p + hh]

        def body(jj, carry, qh=qh, bias=bias):
            acc, r_run = carry
            j = i - jj
            start = pl.multiple_of(j * tk, tk)
            kt = kt_ref[:, pl.ds(start, tk)]
            z = jnp.dot(qh, kt, preferred_element_type=F32) + bias
            mask = (col + j * tk) < (row + i * tq)
            w, r_new = _sb_tile(z, mask, tri, r_run)
            acc = acc + jnp.dot(w.astype(BF16), v_ref[pl.ds(start, tk), :], preferred_element_type=F32)
            return acc, r_new

        acc, _ = lax.fori_loop(0, i + 1, body, (jnp.zeros((tq, LANES), F32), jnp.zeros((tq, 1), F32)))
        accs.append(acc)
    o_ref[...] = jnp.where(lane < HEAD_DIM, accs[0], accs[1])


def _attn_prompt(q_bf, kt_bf, v_bf, bias, tq):
    b, s, w = q_bf.shape
    return pl.pallas_call(
        functools.partial(_attn_kernel, tq=tq),
        grid=(b, w // LANES, s // tq),
        in_specs=[
            pl.BlockSpec(memory_space=pltpu.SMEM),
            pl.BlockSpec((None, tq, LANES), lambda bb, p, i: (bb, i, p)),
            pl.BlockSpec((None, LANES, s), lambda bb, p, i: (bb, p, 0)),
            pl.BlockSpec((None, s, LANES), lambda bb, p, i: (bb, 0, p)),
        ],
        out_specs=pl.BlockSpec((None, tq, LANES), lambda bb, p, i: (bb, i, p)),
        out_shape=jax.ShapeDtypeStruct((b, s, w), F32),
        compiler_params=_cparams("parallel", "parallel", "arbitrary"),
        name="sb_attn_prompt",
    )(bias, q_bf, kt_bf, v_bf)


def _attn_decode_kernel(pt_ref, bias_ref, q_ref, ktn_ref, vtn_ref, *rest, n_heads, dec_seq, ppb):
    k_refs = rest[:ppb]
    v_refs = rest[ppb:2 * ppb]
    o_ref = rest[2 * ppb]
    acc_ref, r_ref, qbd_ref = rest[2 * ppb + 1:]
    j = pl.program_id(1)
    rows = n_heads * dec_seq
    w_all = n_heads * HEAD_DIM
    slot = lax.broadcasted_iota(jnp.int32, (rows, LANES), 1)
    rid = lax.broadcasted_iota(jnp.int32, (rows, LANES), 0)
    trow = lax.broadcasted_iota(jnp.int32, (LANES, LANES), 0)
    tcol = lax.broadcasted_iota(jnp.int32, (LANES, LANES), 1)
    tri = (trow > tcol).astype(BF16)
    bias = bias_ref[...]

    def page(kt, vt, mask):
        z = jnp.dot(qbd_ref[...], kt.astype(BF16), preferred_element_type=F32) + bias
        w, r_new = _sb_tile(z, mask, tri, r_ref[...])
        r_ref[...] = r_new
        acc_ref[...] += lax.dot_general(w.astype(BF16), vt.astype(BF16), (((1,), (1,)), ((), ())),
                                        preferred_element_type=F32)

    @pl.when(j == 0)
    def _():
        q = q_ref[...]
        lq = lax.broadcasted_iota(jnp.int32, q.shape, 1)
        blocks = [jnp.where((lq >= h * HEAD_DIM) & (lq < (h + 1) * HEAD_DIM), q, 0.0) for h in range(n_heads)]
        qbd_ref[...] = jnp.concatenate(blocks, axis=0).astype(BF16)
        acc_ref[...] = jnp.zeros_like(acc_ref)
        r_ref[...] = jnp.zeros_like(r_ref)
        page(ktn_ref[...], vtn_ref[...], slot < (rid & (dec_seq - 1)))

    all_keys = slot >= 0
    for m in range(ppb):
        page(k_refs[m][...].reshape(w_all, LANES), v_refs[m][...].reshape(w_all, LANES), all_keys)

    @pl.when(j == pl.num_programs(1) - 1)
    def _():
        acc = acc_ref[...]
        la = lax.broadcasted_iota(jnp.int32, (dec_seq, w_all), 1)
        out = jnp.zeros((dec_seq, w_all), F32)
        for h in range(n_heads):
            sel = (la >= h * HEAD_DIM) & (la < (h + 1) * HEAD_DIM)
            out = jnp.where(sel, acc[h * dec_seq:(h + 1) * dec_seq, :], out)
        o_ref[...] = out


def _attn_decode(q, ktn, vtn, ck_t, cv_t, layer, page_table, bias_rows, ppb):
    bs, t, w_all = q.shape
    n_heads = w_all // HEAD_DIM
    n_pages = page_table.shape[1]
    steps = n_pages // ppb
    rows = n_heads * t

    def page_spec(m):
        return pl.BlockSpec((None, None, n_heads, HEAD_DIM, LANES),
                            lambda b, j, pt, m=m: (layer, pt[b, n_pages - 1 - (j * ppb + m)], 0, 0, 0))

    grid_spec = pltpu.PrefetchScalarGridSpec(
        num_scalar_prefetch=1,
        grid=(bs, steps),
        in_specs=[
            pl.BlockSpec((rows, 1), lambda b, j, pt: (0, 0)),
            pl.BlockSpec((None, t, w_all), lambda b, j, pt: (b, 0, 0)),
            pl.BlockSpec((None, w_all, LANES), lambda b, j, pt: (b, 0, 0)),
            pl.BlockSpec((None, w_all, LANES), lambda b, j, pt: (b, 0, 0)),
        ] + [page_spec(m) for m in range(ppb)] + [page_spec(m) for m in range(ppb)],
        out_specs=pl.BlockSpec((None, t, w_all), lambda b, j, pt: (b, 0, 0)),
        scratch_shapes=[pltpu.VMEM((rows, w_all), F32), pltpu.VMEM((rows, 1), F32), pltpu.VMEM((rows, w_all), BF16)],
    )
    return pl.pallas_call(
        functools.partial(_attn_decode_kernel, n_heads=n_heads, dec_seq=t, ppb=ppb),
        grid_spec=grid_spec,
        out_shape=jax.ShapeDtypeStruct((bs, t, w_all), F32),
        compiler_params=_cparams("parallel", "arbitrary"),
        name="sb_attn_decode",
    )(page_table, bias_rows, q, ktn, vtn, *([ck_t] * ppb), *([cv_t] * ppb))


def _s5_disc_kernel(are_ref, aim_ref, ldt_ref, bre_ref, bim_ref, abr_ref, abi_ref, bbr_ref, bbi_ref):
    dt = jnp.exp(ldt_ref[...])
    lam_re = jnp.minimum(are_ref[...], -1e-4)
    lam_im = aim_ref[...]
    mag = jnp.exp(lam_re * dt)
    ab_re = mag * jnp.cos(lam_im * dt)
    ab_im = mag * jnp.sin(lam_im * dt)
    den = lam_re * lam_re + lam_im * lam_im
    nr = ab_re - 1.0
    f_re = (nr * lam_re + ab_im * lam_im) / den
    f_im = (ab_im * lam_re - nr * lam_im) / den
    abr_ref[...] = ab_re
    abi_ref[...] = ab_im
    br = bre_ref[...]
    bi = bim_ref[...]
    bbr_ref[...] = f_re * br - f_im * bi
    bbi_ref[...] = f_re * bi + f_im * br


def _s5_disc(a_re, a_im, log_dt, b_re_t, b_im_t):
    g, _, p = a_re.shape
    ch = b_re_t.shape[1]
    return pl.pallas_call(
        _s5_disc_kernel,
        out_shape=[jax.ShapeDtypeStruct((g, 1, p), F32)] * 2 + [jax.ShapeDtypeStruct((g, ch, p), F32)] * 2,
        name="s5_disc",
    )(a_re, a_im, log_dt, b_re_t, b_im_t)


def _cmul(ar, ai, xr, xi):
    return ar * xr - ai * xi, ar * xi + ai * xr


def _s5_kernel(u_ref, x0r_ref, x0i_ref, a1r_ref, a1i_ref, wb_ref, wc_ref, d_ref, wg_ref,
               y_ref, fr_ref, fi_ref, st_r, st_i, cr_ref, ci_ref, xs_ref, *, nb, tp):
    step = pl.program_id(0)
    tr = nb * tp
    rows = u_ref.shape[0]
    ns = a1r_ref.shape[1]

    @pl.when(step == 0)
    def _():
        pad = jnp.zeros((tr - nb, ns), F32)
        st_r[...] = jnp.concatenate([pad, x0r_ref[...]], axis=0) if tp > 1 else x0r_ref[...]
        st_i[...] = jnp.concatenate([pad, x0i_ref[...]], axis=0) if tp > 1 else x0i_ref[...]

    u = u_ref[...]
    bu = _bdot(u, wb_ref[...])
    bur, bui = bu[:, :ns], bu[:, ns:]
    a1r, a1i = a1r_ref[...], a1i_ref[...]
    rid = lax.broadcasted_iota(jnp.int32, (rows, ns), 0)
    if tp > 1:
        sr = jnp.where(rid < nb, 0.0, pltpu.roll(bur, nb, axis=0))
        si = jnp.where(rid < nb, 0.0, pltpu.roll(bui, nb, axis=0))
        pr, pi = _cmul(a1r, a1i, sr, si)
        cr = bur + pr
        ci = bui + pi
    else:
        cr, ci = bur, bui
    cr_ref[...] = cr
    ci_ref[...] = ci
    tid = lax.broadcasted_iota(jnp.int32, (tr, ns), 0)
    lr, li = st_r[...], st_i[...]
    if tp > 1:
        lr = jnp.where(tid < nb, pltpu.roll(lr, nb, axis=0), 0.0)
        li = jnp.where(tid < nb, pltpu.roll(li, nb, axis=0), 0.0)
    fr, fi = _cmul(a1r, a1i, lr, li)
    if tp > 1:
        gr, gi = _cmul(a1r, a1i, pltpu.roll(fr, nb, axis=0), pltpu.roll(fi, nb, axis=0))
        fr = jnp.where(tid < nb, fr, gr)
        fi = jnp.where(tid < nb, fi, gi)
    cr_ref[0:tr, :] = cr_ref[0:tr, :] + fr
    ci_ref[0:tr, :] = ci_ref[0:tr, :] + fi

    apr, api = _cmul(a1r, a1i, a1r, a1i) if tp > 1 else (a1r, a1i)

    def body(k, carry):
        xr, xi = carry
        off = pl.multiple_of(k * tr, tr)
        nr, ni = _cmul(apr, api, xr, xi)
        nr = nr + cr_ref[pl.ds(off, tr), :]
        ni = ni + ci_ref[pl.ds(off, tr), :]
        xs_ref[pl.ds(off, tr), 0:ns] = nr
        xs_ref[pl.ds(off, tr), ns:2 * ns] = ni
        return nr, ni

    xr, xi = lax.fori_loop(0, rows // tr, body, (jnp.zeros((tr, ns), F32), jnp.zeros((tr, ns), F32)))
    st_r[...] = xr
    st_i[...] = xi
    fr_ref[...] = xr
    fi_ref[...] = xi

    y = _bdot(xs_ref[...], wc_ref[...]) + d_ref[...] * u
    g = 0.5 * y * (1.0 + jnp.tanh(math.sqrt(2.0 / math.pi) * (y + 0.044715 * (y * y * y))))
    y_ref[...] = g * _sigmoid(_bdot(g, wg_ref[...]))


def _s5(u_tm, x0r, x0i, a1r, a1i, wb, wc, d_skip, w_glu, nb, ts):
    n, c = u_tm.shape
    ns = a1r.shape[1]
    tp = max(1, SUBLANES // nb)
    tr = nb * tp
    rows = ts * nb
    return pl.pallas_call(
        functools.partial(_s5_kernel, nb=nb, tp=tp),
        grid=(n // rows,),
        in_specs=[
            pl.BlockSpec((rows, c), lambda i: (i, 0)),
            _full((nb, ns)), _full((nb, ns)),
            _full((1, ns)), _full((1, ns)),
            _full(wb.shape), _full(wc.shape), _full((1, c)), _full(w_glu.shape),
        ],
        out_specs=[pl.BlockSpec((rows, c), lambda i: (i, 0)), _full((tr, ns)), _full((tr, ns))],
        out_shape=[jax.ShapeDtypeStruct((n, c), F32), jax.ShapeDtypeStruct((tr, ns), F32),
                   jax.ShapeDtypeStruct((tr, ns), F32)],
        scratch_shapes=[pltpu.VMEM((tr, ns), F32), pltpu.VMEM((tr, ns), F32),
                        pltpu.VMEM((rows, ns), F32), pltpu.VMEM((rows, ns), F32),
                        pltpu.VMEM((rows, 2 * ns), F32)],
        compiler_params=_cparams("arbitrary"),
        name="s5_scan",
    )(u_tm, x0r, x0i, a1r, a1i, wb, wc, d_skip, w_glu)


def _gdn_kernel(*refs, n_heads, ch, n_seq, carry_conv):
    if carry_conv:
        (xc_ref, halo_ref, z_ref, ab_ref, wconv_ref, alog_ref, dtb_ref, gn_ref, hm_ref, ex_ref, exw_ref, s0_ref,
         oc_ref, sfin_ref, s_scr, tail_scr, x_scr, ac_scr, qd_scr, kd_scr, egl_scr, o_scr) = refs
    else:
        (xc_ref, sh1_ref, sh2_ref, sh3_ref, z_ref, ab_ref, wconv_ref, alog_ref, dtb_ref, gn_ref, hm_ref, ex_ref,
         exw_ref, s0_ref, oc_ref, sfin_ref, s_scr, x_scr, ac_scr, qd_scr, kd_scr, egl_scr, o_scr) = refs
    step = pl.program_id(1)
    r = xc_ref.shape[0]
    w_all = n_heads * HEAD_DIM
    nc = r // (n_seq * ch)

    @pl.when(step == 0)
    def _():
        s_scr[...] = s0_ref[...]
        if carry_conv:
            tail_scr[...] = halo_ref[...]

    x = xc_ref[...]
    if carry_conv:
        ext = jnp.concatenate([tail_scr[...], x], axis=0)
        sh = [ext[SUBLANES - j:SUBLANES - j + r] for j in (1, 2, 3)]
        tail_scr[...] = x[r - SUBLANES:, :]
    else:
        sh = [sh1_ref[...], sh2_ref[...], sh3_ref[...]]
    wc = wconv_ref[...]
    conv = sh[2] * wc[0:1] + sh[1] * wc[1:2] + sh[0] * wc[2:3] + x * wc[3:4]
    conv = conv * _sigmoid(conv)
    q = conv[:, 0:w_all]
    k = conv[:, w_all:2 * w_all]
    v = conv[:, 2 * w_all:3 * w_all]
    hm = hm_ref[...]
    q = q * lax.rsqrt(_dot2(q * q, hm) + EPS) * (HEAD_DIM ** -0.5)
    k = k * lax.rsqrt(_dot2(k * k, hm) + EPS)

    ab = ab_ref[...]
    g = -jnp.exp(alog_ref[...]) * _softplus(ab + dtb_ref[...])
    beta = _sigmoid(ab)
    ri = lax.broadcasted_iota(jnp.int32, (r, r), 0)
    ci = lax.broadcasted_iota(jnp.int32, (r, r), 1)
    lg = int(math.log2(ch))
    same = (ri >> lg) == (ci >> lg)
    lower = same & (ci <= ri)
    strict = same & (ci < ri)
    g_cum = _dot2_left(lower.astype(BF16), g)
    g_tot = _dot2_left(same.astype(BF16), g)
    ex = ex_ref[...]
    gc_e = _dot2(g_cum, ex)
    gl_e = _dot2(g_tot, ex)
    beta_e = _dot2(pltpu.roll(beta, LANES - n_heads, axis=1), ex)
    gc_w = _dot2(g_cum, exw_ref[...])
    g_t = g_cum.T

    e_gc = jnp.exp(gc_e)
    kb = k * beta_e
    xall = jnp.concatenate([v * beta_e, kb * e_gc], axis=1)
    qd_scr[...] = q * e_gc
    kd_scr[...] = k * jnp.exp(gl_e - gc_e)
    egl_scr[...] = jnp.exp(gl_e)
    selm = ((lax.broadcasted_iota(jnp.int32, (r, LANES), 0) & (ch - 1))
            == lax.broadcasted_iota(jnp.int32, (r, LANES), 1)).astype(BF16)

    eye = (ri == ci).astype(F32)
    level_masks = []
    for ls in range(lg):
        level_masks.append((((ri >> (ls + 1)) == (ci >> (ls + 1))) & (((ri >> ls) & 1) == 1)
                            & (((ci >> ls) & 1) == 0)))
    for h in range(n_heads):
        hs = slice(h * HEAD_DIM, (h + 1) * HEAD_DIM)
        kh = k[:, hs].astype(BF16)
        kk = lax.dot_general(kb[:, hs].astype(BF16), kh, (((1,), (1,)), ((), ())), preferred_element_type=F32)
        qk = lax.dot_general(q[:, hs].astype(BF16), kh, (((1,), (1,)), ((), ())), preferred_element_type=F32)
        diff = gc_w[:, h * r:(h + 1) * r] - g_t[h:h + 1, :]
        decay = jnp.exp(jnp.where(lower, diff, 0.0))
        m = jnp.where(strict, kk * decay, 0.0)
        attn = jnp.where(lower, qk * decay, 0.0)
        ac_scr[h] = jnp.dot(attn.astype(BF16), selm, preferred_element_type=F32)
        xh = jnp.concatenate([xall[:, hs], xall[:, w_all + h * HEAD_DIM:w_all + (h + 1) * HEAD_DIM]], axis=1)
        t_inv = eye - jnp.where(level_masks[0], m, 0.0)
        for ls in range(1, lg):
            tb = t_inv.astype(BF16)
            y = jnp.dot(tb, jnp.where(level_masks[ls], m, 0.0).astype(BF16), preferred_element_type=F32)
            t_inv = t_inv - jnp.dot(y.astype(BF16), tb, preferred_element_type=F32)
        x_scr[h] = _dot2(t_inv, xh.astype(BF16))

    def seq_body(s, _):
        for h in range(n_heads):
            hs = slice(h * HEAD_DIM, (h + 1) * HEAD_DIM)
            st = s_scr[s, h]
            for c in range(nc):
                off = pl.multiple_of((s * nc + c) * ch, ch)
                rows = pl.ds(off, ch)
                xh = x_scr[h, rows, :]
                stb = st.astype(BF16)
                v_new = xh[:, 0:HEAD_DIM] - jnp.dot(xh[:, HEAD_DIM:].astype(BF16), stb, preferred_element_type=F32)
                vnb = v_new.astype(BF16)
                o = jnp.dot(qd_scr[rows, hs].astype(BF16), stb, preferred_element_type=F32)
                o = o + jnp.dot(ac_scr[h, rows, 0:ch].astype(BF16), vnb, preferred_element_type=F32)
                o_scr[rows, hs] = o
                st = st * egl_scr[pl.ds(off, 1), hs] + lax.dot_general(
                    kd_scr[rows, hs].astype(BF16), vnb, (((0,), (0,)), ((), ())), preferred_element_type=F32)
            s_scr[s, h] = st
        return 0

    lax.fori_loop(0, n_seq, seq_body, 0)

    o = o_scr[...]
    o = o * lax.rsqrt(_dot2(o * o, hm) * (1.0 / HEAD_DIM) + EPS) * gn_ref[...]
    zz = z_ref[...]
    oc_ref[...] = o * (zz * _sigmoid(zz))

    @pl.when(step == pl.num_programs(1) - 1)
    def _():
        sfin_ref[...] = s_scr[...]


def _gdn(xc, shifts, halo, z, ab, wconv, alog, dtb, gn, hm, ex, s0, n_batch, rb, ch, n_seq):
    n, w3 = xc.shape
    w_all = w3 // 3
    n_heads = w_all // HEAD_DIM
    per = n // n_batch // rb
    carry_conv = shifts is None
    exw = jnp.repeat(jnp.eye(LANES, n_heads, dtype=BF16), rb, axis=1)
    row_spec = lambda w: pl.BlockSpec((rb, w), lambda b, i: (b * per + i, 0))
    st_spec = pl.BlockSpec((n_seq, n_heads, HEAD_DIM, HEAD_DIM), lambda b, i: (b, 0, 0, 0))
    ins = [xc]
    specs = [row_spec(w3)]
    if carry_conv:
        ins.append(halo)
        specs.append(pl.BlockSpec((None, SUBLANES, w3), lambda b, i: (b, 0, 0)))
    else:
        ins += list(shifts)
        specs += [row_spec(w3)] * 3
    ins += [z, ab, wconv, alog, dtb, gn, hm, ex, exw, s0]
    specs += [row_spec(w_all), row_spec(LANES), _full(wconv.shape), _full((1, LANES)), _full((1, LANES)),
              _full((1, w_all)), _full(hm.shape), _full(ex.shape), _full(exw.shape), st_spec]
    scratch = [pltpu.VMEM((n_seq, n_heads, HEAD_DIM, HEAD_DIM), F32)]
    if carry_conv:
        scratch.append(pltpu.VMEM((SUBLANES, w3), F32))
    scratch += [pltpu.VMEM((n_heads, rb, 2 * HEAD_DIM), F32), pltpu.VMEM((n_heads, rb, LANES), F32),
                pltpu.VMEM((rb, w_all), F32), pltpu.VMEM((rb, w_all), F32), pltpu.VMEM((rb, w_all), F32),
                pltpu.VMEM((rb, w_all), F32)]
    return pl.pallas_call(
        functools.partial(_gdn_kernel, n_heads=n_heads, ch=ch, n_seq=n_seq, carry_conv=carry_conv),
        grid=(n_batch, per),
        in_specs=specs,
        out_specs=[row_spec(w_all), st_spec],
        out_shape=[jax.ShapeDtypeStruct((n, w_all), F32),
                   jax.ShapeDtypeStruct((n_batch * n_seq, n_heads, HEAD_DIM, HEAD_DIM), F32)],
        scratch_shapes=scratch,
        compiler_params=_cparams("parallel", "arbitrary"),
        name="gdn",
    )(*ins)


def _out_ffn_kernel(x_ref, oa_ref, ob_ref, oc_ref, g1_ref, sc_ref, sh_ref, g2_ref, gain_ref,
                    wo_ref, wg_ref, wu_ref, wd_ref, y_ref, *, wa, wb, n_f):
    mix = _bdot(oa_ref[...], wo_ref[0:wa, :])
    mix = mix + _bdot(ob_ref[...], wo_ref[wa:wa + wb, :])
    mix = mix + _bdot(oc_ref[...], wo_ref[wa + wb:, :])
    x = x_ref[...] + g1_ref[...] * mix
    h = x * lax.rsqrt(jnp.mean(x * x, axis=-1, keepdims=True) + EPS) * gain_ref[...]
    h = h * (1.0 + sc_ref[...]) + sh_ref[...]
    hb = h.astype(BF16)
    f = wg_ref.shape[1]
    fc = f // n_f
    ff = jnp.zeros(x.shape, F32)
    for c in range(n_f):
        gt = jnp.dot(hb, wg_ref[:, c * fc:(c + 1) * fc], preferred_element_type=F32)
        up = jnp.dot(hb, wu_ref[:, c * fc:(c + 1) * fc], preferred_element_type=F32)
        act = (gt * _sigmoid(gt)) * up
        ff = ff + jnp.dot(act.astype(BF16), wd_ref[c * fc:(c + 1) * fc, :], preferred_element_type=F32)
    y_ref[...] = x + g2_ref[...] * ff


def _out_ffn(x, oa, ob, oc, g1, sc2, sh2, g2, gain, wo, wg, wu, wd, tb):
    n, d = x.shape
    nb = n // tb
    per = nb // g1.shape[0]
    r = g1.shape[1]
    f = wg.shape[1]
    n_f = 2 if (f // 2) % LANES == 0 else 1
    mod_spec = pl.BlockSpec((None, r, d), lambda i: (i // per, 0, 0))
    row = lambda w: pl.BlockSpec((tb, w), lambda i: (i, 0))
    return pl.pallas_call(
        functools.partial(_out_ffn_kernel, wa=oa.shape[1], wb=ob.shape[1], n_f=n_f),
        grid=(nb,),
        in_specs=[row(d), row(oa.shape[1]), row(ob.shape[1]), row(oc.shape[1]),
                  mod_spec, mod_spec, mod_spec, mod_spec, _full((1, d)),
                  _full(wo.shape), _full(wg.shape), _full(wu.shape), _full(wd.shape)],
        out_specs=row(d),
        out_shape=jax.ShapeDtypeStruct((n, d), F32),
        compiler_params=_cparams("parallel"),
        name="out_ffn",
    )(x, oa, ob, oc, g1, sc2, sh2, g2, gain, wo, wg, wu, wd)


def _block_diag(blocks):
    g, a, b = blocks.shape
    eye = jnp.eye(g, dtype=blocks.dtype)
    return (blocks[:, :, None, :] * eye[:, None, :, None]).reshape(g * a, g * b)


def _mods(mod, rows_per_seq, tb):
    d = mod.shape[1] // N_ADA
    parts = [mod[:, i * d:(i + 1) * d] for i in range(N_ADA)]
    if rows_per_seq >= tb:
        return [p[:, None, :] for p in parts]
    rep = [jnp.repeat(p, rows_per_seq, axis=0) for p in parts]
    return [p.reshape(-1, tb, d) for p in rep]


def kernel(x_prompt, x_sample, cache_k, cache_v, state_conv, state_ssm_re, state_ssm_im, state_gdn, page_table, c_prompt, c_sample, w_ada, b_ada, norm_mix, norm_ffn, w_in, sb_qnorm, sb_knorm, sb_bias, ssm_a_re, ssm_a_im, ssm_log_dt, ssm_b_re, ssm_b_im, ssm_c_re, ssm_c_im, ssm_d, ssm_w_glu, gdn_conv, gdn_a_log, gdn_dt_bias, gdn_norm, w_out, ffn_gate, ffn_up, ffn_down):
    bp, seq, d = x_prompt.shape
    bs, dseq, _ = x_sample.shape
    depth = w_ada.shape[0]
    n_sb = sb_bias.shape[1]
    n_gd = gdn_a_log.shape[1]
    n_grp, n_state = ssm_a_re.shape[1:]
    w_sb = n_sb * HEAD_DIM
    w_gd = n_gd * HEAD_DIM
    w_ssm = n_grp * SSM_CH
    sizes = (w_sb, w_sb, w_sb, w_ssm, 3 * w_gd, w_gd)
    offs = [0]
    for s_ in sizes:
        offs.append(offs[-1] + s_)
    offs.append(offs[-1] + LANES)
    offs = tuple(offs)
    in_width = w_in.shape[2]
    ns = n_grp * n_state
    n_p = bp * seq
    n_s = bs * dseq
    page = cache_k.shape[2]

    tb_p = min(512, seq)
    tb_f = min(256, seq)
    tq = min(256, seq)
    rb_p = min(256, seq)
    ch_p = math.gcd(seq, GDN_CHUNK)
    ch_s = math.gcd(dseq, GDN_CHUNK)
    ts_p = min(128, seq)

    w_in_bf = jnp.pad(w_in, ((0, 0), (0, 0), (0, offs[-1] - in_width))).astype(BF16)
    w_out_bf = w_out.astype(BF16)
    wg_bf, wu_bf, wd_bf = ffn_gate.astype(BF16), ffn_up.astype(BF16), ffn_down.astype(BF16)
    hm_sb = _block_diag(jnp.ones((n_sb, HEAD_DIM, HEAD_DIM), BF16))
    hm_gd = _block_diag(jnp.ones((n_gd, HEAD_DIM, HEAD_DIM), BF16))
    ex_gd = jnp.repeat(jnp.eye(LANES, n_gd, dtype=BF16), HEAD_DIM, axis=1)
    qg = jnp.tile(sb_qnorm, (1, n_sb))[:, None, :]
    kg = jnp.tile(sb_knorm, (1, n_sb))[:, None, :]
    gng = jnp.tile(gdn_norm, (1, n_gd))[:, None, :]
    alog = jnp.pad(gdn_a_log, ((0, 0), (0, LANES - n_gd)))[:, None, :]
    dtb = jnp.pad(gdn_dt_bias, ((0, 0), (0, LANES - n_gd)))[:, None, :]
    bias_rows = jnp.repeat(sb_bias, dseq, axis=1)[:, :, None]
    ck_t = jnp.transpose(cache_k, (0, 1, 3, 4, 2))
    cv_t = jnp.transpose(cache_v, (0, 1, 3, 4, 2))

    mod_all = _ada(jnp.concatenate([c_prompt, c_sample], axis=0), w_ada, b_ada)

    xp = x_prompt.reshape(n_p, d)
    xs = x_sample.reshape(n_s, d)
    st_p = [[] for _ in range(6)]
    st_s = [[] for _ in range(6)]
    zero_halo = jnp.zeros((bp, SUBLANES, 3 * w_gd), F32)
    zero_ssm = jnp.zeros((bp, ns), F32)
    zero_gdn = jnp.zeros((bp, n_gd, HEAD_DIM, HEAD_DIM), F32)
    tp_p = max(1, SUBLANES // bp)
    tp_s = max(1, SUBLANES // bs)

    for l in range(depth):
        abr, abi, bbr, bbi = _s5_disc(ssm_a_re[l][:, None, :], ssm_a_im[l][:, None, :], ssm_log_dt[l][:, None, None],
                                      jnp.transpose(ssm_b_re[l], (0, 2, 1)), jnp.transpose(ssm_b_im[l], (0, 2, 1)))
        a1r, a1i = abr.reshape(1, ns), abi.reshape(1, ns)
        wb = jnp.concatenate([_block_diag(bbr), _block_diag(bbi)], axis=1).astype(BF16)
        wc = jnp.concatenate([_block_diag(jnp.transpose(ssm_c_re[l], (0, 2, 1))),
                              -_block_diag(jnp.transpose(ssm_c_im[l], (0, 2, 1)))], axis=0).astype(BF16)
        d_skip = ssm_d[l][None, :]
        wglu_bf = ssm_w_glu[l].astype(BF16)

        mods_p = _mods(mod_all[l, :bp], seq, tb_p)
        mods_pf = _mods(mod_all[l, :bp], seq, tb_f)
        mods_s = _mods(mod_all[l, bp:], dseq, n_s)

        for grp in ("p", "s"):
            if grp == "p":
                x, mods, modsf, tb, tbf = xp, mods_p, mods_pf, tb_p, tb_f
            else:
                x, mods, modsf, tb, tbf = xs, mods_s, mods_s, n_s, n_s
            sh1, sc1, g1, sh2, sc2, g2 = mods
            q_bf, k, v, u, xc, z, ab = _inproj(x, sc1, sh1, norm_mix[l][None, :], w_in_bf[l], qg[l], kg[l], hm_sb,
                                               offs, tb)
            if grp == "p":
                k3 = k.reshape(bp, seq, w_sb)
                oa = _attn_prompt(q_bf.reshape(bp, seq, w_sb), jnp.swapaxes(k3, 1, 2).astype(BF16),
                                  v.reshape(bp, seq, w_sb).astype(BF16), sb_bias[l], tq).reshape(n_p, w_sb)
                nb, steps, x0r, x0i = bp, seq, zero_ssm, zero_ssm
                ts, tp = ts_p, tp_p
            else:
                pad = ((0, 0), (0, 0), (0, page - dseq))
                ktn = jnp.pad(jnp.swapaxes(k.reshape(bs, dseq, w_sb), 1, 2), pad)
                vtn = jnp.pad(jnp.swapaxes(v.reshape(bs, dseq, w_sb), 1, 2), pad)
                oa = _attn_decode(q_bf.astype(F32).reshape(bs, dseq, w_sb), ktn, vtn, ck_t, cv_t, l, page_table,
                                  bias_rows[l], 8 if page_table.shape[1] % 8 == 0 else 1).reshape(n_s, w_sb)
                nb, steps = bs, dseq
                x0r, x0i = state_ssm_re[l].reshape(bs, ns), state_ssm_im[l].reshape(bs, ns)
                ts, tp = dseq, tp_s

            u_tm = u.reshape(nb, steps, w_ssm).swapaxes(0, 1).reshape(steps * nb, w_ssm)
            ob_tm, fr, fi = _s5(u_tm, x0r, x0i, a1r, a1i, wb, wc, d_skip, wglu_bf, nb, ts)
            ob = ob_tm.reshape(steps, nb, w_ssm).swapaxes(0, 1).reshape(steps * nb, w_ssm)
            ssm_re = fr[-nb:].reshape(nb, n_grp, n_state)
            ssm_im = fi[-nb:].reshape(nb, n_grp, n_state)

            if grp == "p":
                oc, gdn_s = _gdn(xc, None, zero_halo, z, ab, gdn_conv[l], alog[l], dtb[l], gng[l], hm_gd, ex_gd,
                                 zero_gdn, bp, rb_p, ch_p, 1)
                new_buf = xc.reshape(bp, seq, 3 * w_gd)[:, seq - (CONV_WIDTH - 1):]
            else:
                ext = jnp.concatenate([state_conv[l], xc.reshape(bs, dseq, 3 * w_gd)], axis=1)
                shifts = [ext[:, CONV_WIDTH - 1 - j:CONV_WIDTH - 1 - j + dseq].reshape(n_s, 3 * w_gd)
                          for j in (1, 2, 3)]
                oc, gdn_s = _gdn(xc, shifts, None, z, ab, gdn_conv[l], alog[l], dtb[l], gng[l], hm_gd, ex_gd,
                                 state_gdn[l], 1, n_s, ch_s, bs)
                new_buf = ext[:, dseq:]

            x = _out_ffn(x, oa, ob, oc, g1, sc2, sh2, g2, norm_ffn[l][None, :], w_out_bf[l], wg_bf[l], wu_bf[l],
                         wd_bf[l], tbf) if grp == "s" else _out_ffn(
                x, oa, ob, oc, modsf[2], modsf[4], modsf[3], modsf[5], norm_ffn[l][None, :], w_out_bf[l], wg_bf[l],
                wu_bf[l], wd_bf[l], tbf)
            bn = bp if grp == "p" else bs
            sq = seq if grp == "p" else dseq
            new = (k.reshape(bn, sq, n_sb, HEAD_DIM), v.reshape(bn, sq, n_sb, HEAD_DIM), new_buf, ssm_re, ssm_im,
                   gdn_s)
            tgt = st_p if grp == "p" else st_s
            for i_ in range(6):
                tgt[i_].append(new[i_])
            if grp == "p":
                xp = x
            else:
                xs = x

    outs_p = [jnp.stack(t) for t in st_p]
    outs_s = [jnp.stack(t) for t in st_s]
    return (xp.reshape(bp, seq, d), xs.reshape(bs, dseq, d), *outs_p, *outs_s)
```

```python
import functools
import math

import jax
import jax.numpy as jnp
from jax import lax
from jax.experimental import pallas as pl
from jax.experimental.pallas import tpu as pltpu

F32 = jnp.float32
BF16 = jnp.bfloat16
EPS = 1e-6
HEAD_DIM = 64
SSM_CH = 16
SSM_STATE = 64
CONV_WIDTH = 4
GDN_CHUNK = 64
N_ADA = 6
LANES = 128
SUBLANES = 8
VMEM_LIMIT = 56 * 1024 * 1024


def _cparams(*sem):
    return pltpu.CompilerParams(dimension_semantics=sem, vmem_limit_bytes=VMEM_LIMIT)


def _bdot(a, b):
    return jnp.dot(a.astype(BF16), b.astype(BF16), preferred_element_type=F32)


def _split(x):
    hi = x.astype(BF16)
    lo = (x - hi.astype(F32)).astype(BF16)
    return hi, lo


def _dot2(x, m):
    hi, lo = _split(x)
    return jnp.dot(hi, m, preferred_element_type=F32) + jnp.dot(lo, m, preferred_element_type=F32)


def _dot2_left(m, x):
    hi, lo = _split(x)
    return jnp.dot(m, hi, preferred_element_type=F32) + jnp.dot(m, lo, preferred_element_type=F32)


def _sigmoid(x):
    return 1.0 / (1.0 + jnp.exp(-x))


def _softplus(x):
    return jnp.maximum(x, 0.0) + jnp.log(1.0 + jnp.exp(-jnp.abs(x)))


def _full(shape):
    n = len(shape)
    return pl.BlockSpec(shape, lambda *_: (0,) * n)


def _ada_kernel(c_ref, w_ref, b_ref, o_ref):
    c = c_ref[...]
    s = c * _sigmoid(c)
    o_ref[...] = _bdot(s, w_ref[...]) + b_ref[...]


def _ada(c_all, w_ada, b_ada):
    depth, d, n = w_ada.shape
    r = c_all.shape[0]
    tn = 1536 if n % 1536 == 0 else n
    return pl.pallas_call(
        _ada_kernel,
        grid=(depth, n // tn),
        in_specs=[
            pl.BlockSpec((r, d), lambda l, j: (0, 0)),
            pl.BlockSpec((None, d, tn), lambda l, j: (l, 0, j)),
            pl.BlockSpec((None, 1, tn), lambda l, j: (l, 0, j)),
        ],
        out_specs=pl.BlockSpec((None, r, tn), lambda l, j: (l, 0, j)),
        out_shape=jax.ShapeDtypeStruct((depth, r, n), F32),
        compiler_params=_cparams("parallel", "parallel"),
        name="ada_mod",
    )(c_all, w_ada, b_ada.reshape(depth, 1, n))


def _inproj_kernel(x_ref, sc_ref, sh_ref, g_ref, w_ref, qg_ref, kg_ref, hm_ref,
                   q_ref, k_ref, v_ref, u_ref, xc_ref, z_ref, ab_ref, *, offs):
    x = x_ref[...]
    h = x * lax.rsqrt(jnp.mean(x * x, axis=-1, keepdims=True) + EPS) * g_ref[...]
    h = h * (1.0 + sc_ref[...]) + sh_ref[...]
    hb = h.astype(BF16)

    def proj(a, b):
        return jnp.dot(hb, w_ref[:, a:b], preferred_element_type=F32)

    hm = hm_ref[...]

    def headnorm(t, gain):
        ms = _dot2(t * t, hm) * (1.0 / HEAD_DIM)
        return t * lax.rsqrt(ms + EPS) * gain

    o_q, o_k, o_v, o_u, o_xc, o_z, o_ab, o_end = offs
    q_ref[...] = (headnorm(proj(o_q, o_k), qg_ref[...]) * (HEAD_DIM ** -0.5)).astype(BF16)
    k_ref[...] = headnorm(proj(o_k, o_v), kg_ref[...])
    v_ref[...] = proj(o_v, o_u)
    u_ref[...] = proj(o_u, o_xc)
    xc_ref[...] = proj(o_xc, o_z)
    z_ref[...] = proj(o_z, o_ab)
    ab_ref[...] = proj(o_ab, o_end)


def _inproj(x, sc, sh, gain, w_bf, qg, kg, hm, offs, tb):
    n, d = x.shape
    nb = n // tb
    per = nb // sc.shape[0]
    r = sc.shape[1]
    widths = [offs[i + 1] - offs[i] for i in range(7)]
    dts = [BF16] + [F32] * 6
    mod_spec = pl.BlockSpec((None, r, d), lambda i: (i // per, 0, 0))
    return pl.pallas_call(
        functools.partial(_inproj_kernel, offs=offs),
        grid=(nb,),
        in_specs=[
            pl.BlockSpec((tb, d), lambda i: (i, 0)),
            mod_spec, mod_spec,
            _full((1, d)),
            _full(w_bf.shape),
            _full(qg.shape), _full(kg.shape), _full(hm.shape),
        ],
        out_specs=[pl.BlockSpec((tb, w), lambda i: (i, 0)) for w in widths],
        out_shape=[jax.ShapeDtypeStruct((n, w), dt) for w, dt in zip(widths, dts)],
        compiler_params=_cparams("parallel"),
        name="inproj",
    )(x, sc, sh, gain, w_bf, qg, kg, hm)


def _sb_softplus(z, mask):
    sp = _softplus(z)
    return sp if mask is None else jnp.where(mask, sp, 0.0)


def _sb_weights(z, sp, later, mask):
    w = jnp.exp(z - sp - later)
    return w if mask is None else jnp.where(mask, w, 0.0)


def _attn_kernel(bias_ref, q_ref, kt_ref, v_ref, o_ref, *, tq, tk):
    p = pl.program_id(1)
    i = pl.program_id(2)
    nsub = tq // tk
    q2 = q_ref[...]
    lane = lax.broadcasted_iota(jnp.int32, q2.shape, 1)
    row = lax.broadcasted_iota(jnp.int32, (tk, tk), 0)
    col = lax.broadcasted_iota(jnp.int32, (tk, tk), 1)
    tri = (row > col).astype(BF16)
    diag = col < row
    zero = jnp.zeros_like(q2)
    qh = [jnp.where(lane < HEAD_DIM, q2, zero), jnp.where(lane >= HEAD_DIM, q2, zero)]
    bias = [bias_ref[2 * p], bias_ref[2 * p + 1]]

    def tile(qs, j, carry, mask):
        start = pl.multiple_of(j * tk, tk)
        kt = kt_ref[:, pl.ds(start, tk)]
        vv = v_ref[pl.ds(start, tk), :]
        out = []
        for hh in range(2):
            acc, later = carry[hh]
            z = jnp.dot(qs[hh], kt, preferred_element_type=F32) + bias[hh]
            sp = _sb_softplus(z, mask)
            local = jnp.dot(sp.astype(BF16), tri, preferred_element_type=F32)
            w = _sb_weights(z, sp, local + later, mask)
            acc = acc + jnp.dot(w.astype(BF16), vv, preferred_element_type=F32)
            out.append((acc, later + jnp.sum(sp, axis=1, keepdims=True)))
        return tuple(out)

    subs = []
    for s in range(nsub):
        qs = [qh[hh][s * tk:(s + 1) * tk] for hh in range(2)]
        c = ((jnp.zeros((tk, LANES), F32), jnp.zeros((tk, 1), F32)),) * 2
        for jj in range(s, -1, -1):
            c = tile(qs, i * nsub + jj, c, diag if jj == s else None)
        subs.append(c)
    carry = tuple((jnp.concatenate([subs[s][hh][0] for s in range(nsub)], axis=0),
                   jnp.concatenate([subs[s][hh][1] for s in range(nsub)], axis=0)) for hh in range(2))
    carry = lax.fori_loop(0, i * nsub, lambda jj, c: tile(qh, i * nsub - 1 - jj, c, None), carry)
    o_ref[...] = jnp.where(lane < HEAD_DIM, carry[0][0], carry[1][0])


def _attn_prompt(q_bf, kt_bf, v_bf, bias, tq, tk):
    b, s, w = q_bf.shape
    return pl.pallas_call(
        functools.partial(_attn_kernel, tq=tq, tk=tk),
        grid=(b, w // LANES, s // tq),
        in_specs=[
            pl.BlockSpec(memory_space=pltpu.SMEM),
            pl.BlockSpec((None, tq, LANES), lambda bb, p, i: (bb, i, p)),
            pl.BlockSpec((None, LANES, s), lambda bb, p, i: (bb, p, 0)),
            pl.BlockSpec((None, s, LANES), lambda bb, p, i: (bb, 0, p)),
        ],
        out_specs=pl.BlockSpec((None, tq, LANES), lambda bb, p, i: (bb, i, p)),
        out_shape=jax.ShapeDtypeStruct((b, s, w), F32),
        compiler_params=_cparams("parallel", "parallel", "arbitrary"),
        name="sb_attn_prompt",
    )(bias, q_bf, kt_bf, v_bf)


def _attn_decode_kernel(pt_ref, bias_ref, q_ref, ktn_ref, vtn_ref, *rest, n_heads, dec_seq, ppb):
    k_refs = rest[:ppb]
    v_refs = rest[ppb:2 * ppb]
    o_ref = rest[2 * ppb]
    acc_ref, r_ref, qbd_ref = rest[2 * ppb + 1:]
    j = pl.program_id(1)
    rows = n_heads * dec_seq
    w_all = n_heads * HEAD_DIM
    slot = lax.broadcasted_iota(jnp.int32, (rows, LANES), 1)
    rid = lax.broadcasted_iota(jnp.int32, (rows, LANES), 0)
    trow = lax.broadcasted_iota(jnp.int32, (LANES, LANES), 0)
    tcol = lax.broadcasted_iota(jnp.int32, (LANES, LANES), 1)
    tri = (trow > tcol).astype(BF16)
    bias = bias_ref[...]

    def pages(kts, vts, mask, acc, later):
        qbd = qbd_ref[...]
        zs = [jnp.dot(qbd, kt.astype(BF16), preferred_element_type=F32) + bias for kt in kts]
        sps = [_sb_softplus(z, mask) for z in zs]
        locs = [jnp.dot(sp.astype(BF16), tri, preferred_element_type=F32) for sp in sps]
        for z, sp, loc, vt in zip(zs, sps, locs, vts):
            w = _sb_weights(z, sp, loc + later, mask)
            acc = acc + lax.dot_general(w.astype(BF16), vt.astype(BF16), (((1,), (1,)), ((), ())),
                                        preferred_element_type=F32)
            later = later + jnp.sum(sp, axis=1, keepdims=True)
        return acc, later

    @pl.when(j == 0)
    def _():
        q = q_ref[...]
        lq = lax.broadcasted_iota(jnp.int32, q.shape, 1)
        blocks = [jnp.where((lq >= h * HEAD_DIM) & (lq < (h + 1) * HEAD_DIM), q, 0.0) for h in range(n_heads)]
        qbd_ref[...] = jnp.concatenate(blocks, axis=0).astype(BF16)
        acc, later = pages([ktn_ref[...]], [vtn_ref[...]], slot < (rid & (dec_seq - 1)),
                           jnp.zeros(acc_ref.shape, F32), jnp.zeros(r_ref.shape, F32))
        acc_ref[...] = acc
        r_ref[...] = later

    acc, later = pages([k_refs[m][...].reshape(w_all, LANES) for m in range(ppb)],
                       [v_refs[m][...].reshape(w_all, LANES) for m in range(ppb)], None, acc_ref[...], r_ref[...])
    acc_ref[...] = acc
    r_ref[...] = later

    @pl.when(j == pl.num_programs(1) - 1)
    def _():
        acc = acc_ref[...]
        la = lax.broadcasted_iota(jnp.int32, (dec_seq, w_all), 1)
        out = jnp.zeros((dec_seq, w_all), F32)
        for h in range(n_heads):
            sel = (la >= h * HEAD_DIM) & (la < (h + 1) * HEAD_DIM)
            out = jnp.where(sel, acc[h * dec_seq:(h + 1) * dec_seq, :], out)
        o_ref[...] = out


def _attn_decode(q, ktn, vtn, ck_t, cv_t, layer, page_table, bias_rows, ppb):
    bs, t, w_all = q.shape
    n_heads = w_all // HEAD_DIM
    n_pages = page_table.shape[1]
    steps = n_pages // ppb
    rows = n_heads * t

    def page_spec(m):
        return pl.BlockSpec((None, None, n_heads, HEAD_DIM, LANES),
                            lambda b, j, pt, m=m: (layer, pt[b, n_pages - 1 - (j * ppb + m)], 0, 0, 0))

    grid_spec = pltpu.PrefetchScalarGridSpec(
        num_scalar_prefetch=1,
        grid=(bs, steps),
        in_specs=[
            pl.BlockSpec((rows, 1), lambda b, j, pt: (0, 0)),
            pl.BlockSpec((None, t, w_all), lambda b, j, pt: (b, 0, 0)),
            pl.BlockSpec((None, w_all, LANES), lambda b, j, pt: (b, 0, 0)),
            pl.BlockSpec((None, w_all, LANES), lambda b, j, pt: (b, 0, 0)),
        ] + [page_spec(m) for m in range(ppb)] + [page_spec(m) for m in range(ppb)],
        out_specs=pl.BlockSpec((None, t, w_all), lambda b, j, pt: (b, 0, 0)),
        scratch_shapes=[pltpu.VMEM((rows, w_all), F32), pltpu.VMEM((rows, 1), F32), pltpu.VMEM((rows, w_all), BF16)],
    )
    return pl.pallas_call(
        functools.partial(_attn_decode_kernel, n_heads=n_heads, dec_seq=t, ppb=ppb),
        grid_spec=grid_spec,
        out_shape=jax.ShapeDtypeStruct((bs, t, w_all), F32),
        compiler_params=_cparams("parallel", "arbitrary"),
        name="sb_attn_decode",
    )(page_table, bias_rows, q, ktn, vtn, *([ck_t] * ppb), *([cv_t] * ppb))


def _s5_disc_kernel(are_ref, aim_ref, ldt_ref, bre_ref, bim_ref, abr_ref, abi_ref, bbr_ref, bbi_ref):
    dt = jnp.exp(ldt_ref[...])
    lam_re = jnp.minimum(are_ref[...], -1e-4)
    lam_im = aim_ref[...]
    mag = jnp.exp(lam_re * dt)
    ab_re = mag * jnp.cos(lam_im * dt)
    ab_im = mag * jnp.sin(lam_im * dt)
    den = lam_re * lam_re + lam_im * lam_im
    nr = ab_re - 1.0
    f_re = (nr * lam_re + ab_im * lam_im) / den
    f_im = (ab_im * lam_re - nr * lam_im) / den
    abr_ref[...] = ab_re
    abi_ref[...] = ab_im
    br = bre_ref[...]
    bi = bim_ref[...]
    bbr_ref[...] = f_re * br - f_im * bi
    bbi_ref[...] = f_re * bi + f_im * br


def _s5_disc(a_re, a_im, log_dt, b_re_t, b_im_t):
    g, _, p = a_re.shape
    ch = b_re_t.shape[1]
    return pl.pallas_call(
        _s5_disc_kernel,
        out_shape=[jax.ShapeDtypeStruct((g, 1, p), F32)] * 2 + [jax.ShapeDtypeStruct((g, ch, p), F32)] * 2,
        name="s5_disc",
    )(a_re, a_im, log_dt, b_re_t, b_im_t)


def _cmul(ar, ai, xr, xi):
    return ar * xr - ai * xi, ar * xi + ai * xr


def _s5_kernel(u_ref, x0r_ref, x0i_ref, a1r_ref, a1i_ref, wb_ref, wc_ref, d_ref, wg_ref,
               y_ref, fr_ref, fi_ref, st_r, st_i, cr_ref, ci_ref, xs_ref, *, nb, tp):
    step = pl.program_id(0)
    tr = nb * tp
    rows = u_ref.shape[0]
    ns = a1r_ref.shape[1]

    @pl.when(step == 0)
    def _():
        pad = jnp.zeros((tr - nb, ns), F32)
        st_r[...] = jnp.concatenate([pad, x0r_ref[...]], axis=0) if tp > 1 else x0r_ref[...]
        st_i[...] = jnp.concatenate([pad, x0i_ref[...]], axis=0) if tp > 1 else x0i_ref[...]

    u = u_ref[...]
    bu = _bdot(u, wb_ref[...])
    bur, bui = bu[:, :ns], bu[:, ns:]
    a1r, a1i = a1r_ref[...], a1i_ref[...]
    rid = lax.broadcasted_iota(jnp.int32, (rows, ns), 0)
    if tp > 1:
        sr = jnp.where(rid < nb, 0.0, pltpu.roll(bur, nb, axis=0))
        si = jnp.where(rid < nb, 0.0, pltpu.roll(bui, nb, axis=0))
        pr, pi = _cmul(a1r, a1i, sr, si)
        cr = bur + pr
        ci = bui + pi
    else:
        cr, ci = bur, bui
    cr_ref[...] = cr
    ci_ref[...] = ci
    tid = lax.broadcasted_iota(jnp.int32, (tr, ns), 0)
    lr, li = st_r[...], st_i[...]
    if tp > 1:
        lr = jnp.where(tid < nb, pltpu.roll(lr, nb, axis=0), 0.0)
        li = jnp.where(tid < nb, pltpu.roll(li, nb, axis=0), 0.0)
    fr, fi = _cmul(a1r, a1i, lr, li)
    if tp > 1:
        gr, gi = _cmul(a1r, a1i, pltpu.roll(fr, nb, axis=0), pltpu.roll(fi, nb, axis=0))
        fr = jnp.where(tid < nb, fr, gr)
        fi = jnp.where(tid < nb, fi, gi)
    cr_ref[0:tr, :] = cr_ref[0:tr, :] + fr
    ci_ref[0:tr, :] = ci_ref[0:tr, :] + fi

    apr, api = _cmul(a1r, a1i, a1r, a1i) if tp > 1 else (a1r, a1i)

    def body(k, carry):
        xr, xi = carry
        off = pl.multiple_of(k * tr, tr)
        nr, ni = _cmul(apr, api, xr, xi)
        nr = nr + cr_ref[pl.ds(off, tr), :]
        ni = ni + ci_ref[pl.ds(off, tr), :]
        xs_ref[pl.ds(off, tr), 0:ns] = nr
        xs_ref[pl.ds(off, tr), ns:2 * ns] = ni
        return nr, ni

    xr, xi = lax.fori_loop(0, rows // tr, body, (jnp.zeros((tr, ns), F32), jnp.zeros((tr, ns), F32)))
    st_r[...] = xr
    st_i[...] = xi
    fr_ref[...] = xr
    fi_ref[...] = xi

    y = _bdot(xs_ref[...], wc_ref[...]) + d_ref[...] * u
    g = 0.5 * y * (1.0 + jnp.tanh(math.sqrt(2.0 / math.pi) * (y + 0.044715 * (y * y * y))))
    y_ref[...] = g * _sigmoid(_bdot(g, wg_ref[...]))


def _s5(u_tm, x0r, x0i, a1r, a1i, wb, wc, d_skip, w_glu, nb, ts):
    n, c = u_tm.shape
    ns = a1r.shape[1]
    tp = max(1, SUBLANES // nb)
    tr = nb * tp
    rows = ts * nb
    return pl.pallas_call(
        functools.partial(_s5_kernel, nb=nb, tp=tp),
        grid=(n // rows,),
        in_specs=[
            pl.BlockSpec((rows, c), lambda i: (i, 0)),
            _full((nb, ns)), _full((nb, ns)),
            _full((1, ns)), _full((1, ns)),
            _full(wb.shape), _full(wc.shape), _full((1, c)), _full(w_glu.shape),
        ],
        out_specs=[pl.BlockSpec((rows, c), lambda i: (i, 0)), _full((tr, ns)), _full((tr, ns))],
        out_shape=[jax.ShapeDtypeStruct((n, c), F32), jax.ShapeDtypeStruct((tr, ns), F32),
                   jax.ShapeDtypeStruct((tr, ns), F32)],
        scratch_shapes=[pltpu.VMEM((tr, ns), F32), pltpu.VMEM((tr, ns), F32),
                        pltpu.VMEM((rows, ns), F32), pltpu.VMEM((rows, ns), F32),
                        pltpu.VMEM((rows, 2 * ns), F32)],
        compiler_params=_cparams("arbitrary"),
        name="s5_scan",
    )(u_tm, x0r, x0i, a1r, a1i, wb, wc, d_skip, w_glu)


def _gdn_kernel(*refs, n_heads, ch, n_seq, carry_conv):
    if carry_conv:
        (xc_ref, halo_ref, z_ref, ab_ref, wconv_ref, alog_ref, dtb_ref, gn_ref, hm_ref, ex_ref, exw_ref, s0_ref,
         oc_ref, sfin_ref, s_scr, tail_scr, x_scr, ac_scr, qd_scr, kd_scr, egl_scr, o_scr, m_scr, t_scr) = refs
    else:
        (xc_ref, sh1_ref, sh2_ref, sh3_ref, z_ref, ab_ref, wconv_ref, alog_ref, dtb_ref, gn_ref, hm_ref, ex_ref,
         exw_ref, s0_ref, oc_ref, sfin_ref, s_scr, x_scr, ac_scr, qd_scr, kd_scr, egl_scr, o_scr, m_scr,
         t_scr) = refs
    step = pl.program_id(1)
    r = xc_ref.shape[0]
    w_all = n_heads * HEAD_DIM
    nc = r // (n_seq * ch)

    @pl.when(step == 0)
    def _():
        s_scr[...] = s0_ref[...]
        if carry_conv:
            tail_scr[...] = halo_ref[...]

    x = xc_ref[...]
    if carry_conv:
        ext = jnp.concatenate([tail_scr[...], x], axis=0)
        sh = [ext[SUBLANES - j:SUBLANES - j + r] for j in (1, 2, 3)]
        tail_scr[...] = x[r - SUBLANES:, :]
    else:
        sh = [sh1_ref[...], sh2_ref[...], sh3_ref[...]]
    wc = wconv_ref[...]
    conv = sh[2] * wc[0:1] + sh[1] * wc[1:2] + sh[0] * wc[2:3] + x * wc[3:4]
    conv = conv * _sigmoid(conv)
    q = conv[:, 0:w_all]
    k = conv[:, w_all:2 * w_all]
    v = conv[:, 2 * w_all:3 * w_all]
    hm = hm_ref[...]
    q = q * lax.rsqrt(_dot2(q * q, hm) + EPS) * (HEAD_DIM ** -0.5)
    k = k * lax.rsqrt(_dot2(k * k, hm) + EPS)

    ab = ab_ref[...]
    g = -jnp.exp(alog_ref[...]) * _softplus(ab + dtb_ref[...])
    beta = _sigmoid(ab)
    ri = lax.broadcasted_iota(jnp.int32, (r, r), 0)
    ci = lax.broadcasted_iota(jnp.int32, (r, r), 1)
    lg = int(math.log2(ch))
    same = (ri >> lg) == (ci >> lg)
    lower = same & (ci <= ri)
    strict = same & (ci < ri)
    g_cum = _dot2_left(lower.astype(BF16), g)
    g_tot = _dot2_left(same.astype(BF16), g)
    ex = ex_ref[...]
    gc_e = _dot2(g_cum, ex)
    gl_e = _dot2(g_tot, ex)
    beta_e = _dot2(pltpu.roll(beta, LANES - n_heads, axis=1), ex)
    gc_w = _dot2(g_cum, exw_ref[...])
    g_t = g_cum.T

    e_gc = jnp.exp(gc_e)
    kb = k * beta_e
    xall = jnp.concatenate([v * beta_e, kb * e_gc], axis=1)
    qd_scr[...] = q * e_gc
    kd_scr[...] = k * jnp.exp(gl_e - gc_e)
    egl_scr[...] = jnp.exp(gl_e)
    selm = ((lax.broadcasted_iota(jnp.int32, (r, LANES), 0) & (ch - 1))
            == lax.broadcasted_iota(jnp.int32, (r, LANES), 1)).astype(BF16)

    def level_mask(ls):
        return (((ri >> (ls + 1)) == (ci >> (ls + 1))) & (((ri >> ls) & 1) == 1) & (((ci >> ls) & 1) == 0))

    eye = (ri == ci).astype(F32)
    heads = range(n_heads)
    hsl = [slice(h * HEAD_DIM, (h + 1) * HEAD_DIM) for h in heads]
    nt = (((1,), (1,)), ((), ()))
    khs = [k[:, hs].astype(BF16) for hs in hsl]
    kks = [lax.dot_general(kb[:, hs].astype(BF16), kh, nt, preferred_element_type=F32) for hs, kh in zip(hsl, khs)]
    qks = [lax.dot_general(q[:, hs].astype(BF16), kh, nt, preferred_element_type=F32) for hs, kh in zip(hsl, khs)]
    mask0 = level_mask(0)
    attns = []
    for h in heads:
        diff = gc_w[:, h * r:(h + 1) * r] - g_t[h:h + 1, :]
        decay = jnp.exp(jnp.where(lower, diff, 0.0))
        m = jnp.where(strict, kks[h] * decay, 0.0)
        m_scr[h] = m
        t_scr[h] = eye - jnp.where(mask0, m, 0.0)
        attns.append(jnp.where(lower, qks[h] * decay, 0.0).astype(BF16))
    for h in heads:
        ac_scr[h] = jnp.dot(attns[h], selm, preferred_element_type=F32)
    for ls in range(1, lg):
        mask = level_mask(ls)
        tbs = [t_scr[h].astype(BF16) for h in heads]
        ys = [jnp.dot(tbs[h], jnp.where(mask, m_scr[h], 0.0).astype(BF16), preferred_element_type=F32)
              for h in heads]
        for h in heads:
            t_scr[h] = t_scr[h] - jnp.dot(ys[h].astype(BF16), tbs[h], preferred_element_type=F32)
    for h in heads:
        xh = jnp.concatenate([xall[:, hsl[h]], xall[:, w_all + h * HEAD_DIM:w_all + (h + 1) * HEAD_DIM]], axis=1)
        x_scr[h] = _dot2(t_scr[h], xh.astype(BF16))

    def seq_body(s, _):
        sts = [s_scr[s, h] for h in heads]
        for c in range(nc):
            off = pl.multiple_of((s * nc + c) * ch, ch)
            rows = pl.ds(off, ch)
            for h in heads:
                hs = hsl[h]
                xh = x_scr[h, rows, :]
                stb = sts[h].astype(BF16)
                v_new = xh[:, 0:HEAD_DIM] - jnp.dot(xh[:, HEAD_DIM:].astype(BF16), stb, preferred_element_type=F32)
                vnb = v_new.astype(BF16)
                o = jnp.dot(qd_scr[rows, hs].astype(BF16), stb, preferred_element_type=F32)
                o = o + jnp.dot(ac_scr[h, rows, 0:ch].astype(BF16), vnb, preferred_element_type=F32)
                o_scr[rows, hs] = o
                sts[h] = sts[h] * egl_scr[pl.ds(off, 1), hs] + lax.dot_general(
                    kd_scr[rows, hs].astype(BF16), vnb, (((0,), (0,)), ((), ())), preferred_element_type=F32)
        for h in heads:
            s_scr[s, h] = sts[h]
        return 0

    lax.fori_loop(0, n_seq, seq_body, 0)

    o = o_scr[...]
    o = o * lax.rsqrt(_dot2(o * o, hm) * (1.0 / HEAD_DIM) + EPS) * gn_ref[...]
    zz = z_ref[...]
    oc_ref[...] = o * (zz * _sigmoid(zz))

    @pl.when(step == pl.num_programs(1) - 1)
    def _():
        sfin_ref[...] = s_scr[...]


def _gdn(xc, shifts, halo, z, ab, wconv, alog, dtb, gn, hm, ex, s0, n_batch, rb, ch, n_seq):
    n, w3 = xc.shape
    w_all = w3 // 3
    n_heads = w_all // HEAD_DIM
    per = n // n_batch // rb
    carry_conv = shifts is None
    exw = jnp.repeat(jnp.eye(LANES, n_heads, dtype=BF16), rb, axis=1)
    row_spec = lambda w: pl.BlockSpec((rb, w), lambda b, i: (b * per + i, 0))
    st_spec = pl.BlockSpec((n_seq, n_heads, HEAD_DIM, HEAD_DIM), lambda b, i: (b, 0, 0, 0))
    ins = [xc]
    specs = [row_spec(w3)]
    if carry_conv:
        ins.append(halo)
        specs.append(pl.BlockSpec((None, SUBLANES, w3), lambda b, i: (b, 0, 0)))
    else:
        ins += list(shifts)
        specs += [row_spec(w3)] * 3
    ins += [z, ab, wconv, alog, dtb, gn, hm, ex, exw, s0]
    specs += [row_spec(w_all), row_spec(LANES), _full(wconv.shape), _full((1, LANES)), _full((1, LANES)),
              _full((1, w_all)), _full(hm.shape), _full(ex.shape), _full(exw.shape), st_spec]
    scratch = [pltpu.VMEM((n_seq, n_heads, HEAD_DIM, HEAD_DIM), F32)]
    if carry_conv:
        scratch.append(pltpu.VMEM((SUBLANES, w3), F32))
    scratch += [pltpu.VMEM((n_heads, rb, 2 * HEAD_DIM), F32), pltpu.VMEM((n_heads, rb, LANES), F32),
                pltpu.VMEM((rb, w_all), F32), pltpu.VMEM((rb, w_all), F32), pltpu.VMEM((rb, w_all), F32),
                pltpu.VMEM((rb, w_all), F32), pltpu.VMEM((n_heads, rb, rb), F32), pltpu.VMEM((n_heads, rb, rb), F32)]
    return pl.pallas_call(
        functools.partial(_gdn_kernel, n_heads=n_heads, ch=ch, n_seq=n_seq, carry_conv=carry_conv),
        grid=(n_batch, per),
        in_specs=specs,
        out_specs=[row_spec(w_all), st_spec],
        out_shape=[jax.ShapeDtypeStruct((n, w_all), F32),
                   jax.ShapeDtypeStruct((n_batch * n_seq, n_heads, HEAD_DIM, HEAD_DIM), F32)],
        scratch_shapes=scratch,
        compiler_params=_cparams("parallel", "arbitrary"),
        name="gdn",
    )(*ins)


def _out_ffn_kernel(x_ref, oa_ref, ob_ref, oc_ref, g1_ref, sc_ref, sh_ref, g2_ref, gain_ref,
                    wo_ref, wg_ref, wu_ref, wd_ref, y_ref, *, wa, wb, n_f):
    mix = _bdot(oa_ref[...], wo_ref[0:wa, :])
    mix = mix + _bdot(ob_ref[...], wo_ref[wa:wa + wb, :])
    mix = mix + _bdot(oc_ref[...], wo_ref[wa + wb:, :])
    x = x_ref[...] + g1_ref[...] * mix
    h = x * lax.rsqrt(jnp.mean(x * x, axis=-1, keepdims=True) + EPS) * gain_ref[...]
    h = h * (1.0 + sc_ref[...]) + sh_ref[...]
    hb = h.astype(BF16)
    f = wg_ref.shape[1]
    fc = f // n_f
    ff = jnp.zeros(x.shape, F32)
    for c in range(n_f):
        gt = jnp.dot(hb, wg_ref[:, c * fc:(c + 1) * fc], preferred_element_type=F32)
        up = jnp.dot(hb, wu_ref[:, c * fc:(c + 1) * fc], preferred_element_type=F32)
        act = (gt * _sigmoid(gt)) * up
        ff = ff + jnp.dot(act.astype(BF16), wd_ref[c * fc:(c + 1) * fc, :], preferred_element_type=F32)
    y_ref[...] = x + g2_ref[...] * ff


def _out_ffn(x, oa, ob, oc, g1, sc2, sh2, g2, gain, wo, wg, wu, wd, tb):
    n, d = x.shape
    nb = n // tb
    per = nb // g1.shape[0]
    r = g1.shape[1]
    f = wg.shape[1]
    n_f = 2 if (f // 2) % LANES == 0 else 1
    mod_spec = pl.BlockSpec((None, r, d), lambda i: (i // per, 0, 0))
    row = lambda w: pl.BlockSpec((tb, w), lambda i: (i, 0))
    return pl.pallas_call(
        functools.partial(_out_ffn_kernel, wa=oa.shape[1], wb=ob.shape[1], n_f=n_f),
        grid=(nb,),
        in_specs=[row(d), row(oa.shape[1]), row(ob.shape[1]), row(oc.shape[1]),
                  mod_spec, mod_spec, mod_spec, mod_spec, _full((1, d)),
                  _full(wo.shape), _full(wg.shape), _full(wu.shape), _full(wd.shape)],
        out_specs=row(d),
        out_shape=jax.ShapeDtypeStruct((n, d), F32),
        compiler_params=_cparams("parallel"),
        name="out_ffn",
    )(x, oa, ob, oc, g1, sc2, sh2, g2, gain, wo, wg, wu, wd)


def _block_diag(blocks):
    g, a, b = blocks.shape
    eye = jnp.eye(g, dtype=blocks.dtype)
    return (blocks[:, :, None, :] * eye[:, None, :, None]).reshape(g * a, g * b)


def _mods(mod, rows_per_seq, tb):
    d = mod.shape[1] // N_ADA
    parts = [mod[:, i * d:(i + 1) * d] for i in range(N_ADA)]
    if rows_per_seq >= tb:
        return [p[:, None, :] for p in parts]
    rep = [jnp.repeat(p, rows_per_seq, axis=0) for p in parts]
    return [p.reshape(-1, tb, d) for p in rep]


def kernel(x_prompt, x_sample, cache_k, cache_v, state_conv, state_ssm_re, state_ssm_im, state_gdn, page_table, c_prompt, c_sample, w_ada, b_ada, norm_mix, norm_ffn, w_in, sb_qnorm, sb_knorm, sb_bias, ssm_a_re, ssm_a_im, ssm_log_dt, ssm_b_re, ssm_b_im, ssm_c_re, ssm_c_im, ssm_d, ssm_w_glu, gdn_conv, gdn_a_log, gdn_dt_bias, gdn_norm, w_out, ffn_gate, ffn_up, ffn_down):
    bp, seq, d = x_prompt.shape
    bs, dseq, _ = x_sample.shape
    depth = w_ada.shape[0]
    n_sb = sb_bias.shape[1]
    n_gd = gdn_a_log.shape[1]
    n_grp, n_state = ssm_a_re.shape[1:]
    w_sb = n_sb * HEAD_DIM
    w_gd = n_gd * HEAD_DIM
    w_ssm = n_grp * SSM_CH
    sizes = (w_sb, w_sb, w_sb, w_ssm, 3 * w_gd, w_gd)
    offs = [0]
    for s_ in sizes:
        offs.append(offs[-1] + s_)
    offs.append(offs[-1] + LANES)
    offs = tuple(offs)
    in_width = w_in.shape[2]
    ns = n_grp * n_state
    n_p = bp * seq
    n_s = bs * dseq
    page = cache_k.shape[2]

    tb_p = min(512, seq)
    tb_f = min(256, seq)
    tq = min(512, seq)
    tk = min(256, seq)
    rb_p = min(256, seq)
    ch_p = math.gcd(seq, GDN_CHUNK)
    ch_s = math.gcd(dseq, GDN_CHUNK)
    ts_p = min(128, seq)

    w_in_bf = jnp.pad(w_in, ((0, 0), (0, 0), (0, offs[-1] - in_width))).astype(BF16)
    w_out_bf = w_out.astype(BF16)
    wg_bf, wu_bf, wd_bf = ffn_gate.astype(BF16), ffn_up.astype(BF16), ffn_down.astype(BF16)
    hm_sb = _block_diag(jnp.ones((n_sb, HEAD_DIM, HEAD_DIM), BF16))
    hm_gd = _block_diag(jnp.ones((n_gd, HEAD_DIM, HEAD_DIM), BF16))
    ex_gd = jnp.repeat(jnp.eye(LANES, n_gd, dtype=BF16), HEAD_DIM, axis=1)
    qg = jnp.tile(sb_qnorm, (1, n_sb))[:, None, :]
    kg = jnp.tile(sb_knorm, (1, n_sb))[:, None, :]
    gng = jnp.tile(gdn_norm, (1, n_gd))[:, None, :]
    alog = jnp.pad(gdn_a_log, ((0, 0), (0, LANES - n_gd)))[:, None, :]
    dtb = jnp.pad(gdn_dt_bias, ((0, 0), (0, LANES - n_gd)))[:, None, :]
    bias_rows = jnp.repeat(sb_bias, dseq, axis=1)[:, :, None]
    ck_t = jnp.transpose(cache_k, (0, 1, 3, 4, 2))
    cv_t = jnp.transpose(cache_v, (0, 1, 3, 4, 2))

    mod_all = _ada(jnp.concatenate([c_prompt, c_sample], axis=0), w_ada, b_ada)

    xp = x_prompt.reshape(n_p, d)
    xs = x_sample.reshape(n_s, d)
    st_p = [[] for _ in range(6)]
    st_s = [[] for _ in range(6)]
    zero_halo = jnp.zeros((bp, SUBLANES, 3 * w_gd), F32)
    zero_ssm = jnp.zeros((bp, ns), F32)
    zero_gdn = jnp.zeros((bp, n_gd, HEAD_DIM, HEAD_DIM), F32)
    tp_p = max(1, SUBLANES // bp)
    tp_s = max(1, SUBLANES // bs)

    for l in range(depth):
        abr, abi, bbr, bbi = _s5_disc(ssm_a_re[l][:, None, :], ssm_a_im[l][:, None, :], ssm_log_dt[l][:, None, None],
                                      jnp.transpose(ssm_b_re[l], (0, 2, 1)), jnp.transpose(ssm_b_im[l], (0, 2, 1)))
        a1r, a1i = abr.reshape(1, ns), abi.reshape(1, ns)
        wb = jnp.concatenate([_block_diag(bbr), _block_diag(bbi)], axis=1).astype(BF16)
        wc = jnp.concatenate([_block_diag(jnp.transpose(ssm_c_re[l], (0, 2, 1))),
                              -_block_diag(jnp.transpose(ssm_c_im[l], (0, 2, 1)))], axis=0).astype(BF16)
        d_skip = ssm_d[l][None, :]
        wglu_bf = ssm_w_glu[l].astype(BF16)

        mods_p = _mods(mod_all[l, :bp], seq, tb_p)
        mods_pf = _mods(mod_all[l, :bp], seq, tb_f)
        mods_s = _mods(mod_all[l, bp:], dseq, n_s)

        for grp in ("p", "s"):
            if grp == "p":
                x, mods, modsf, tb, tbf = xp, mods_p, mods_pf, tb_p, tb_f
            else:
                x, mods, modsf, tb, tbf = xs, mods_s, mods_s, n_s, n_s
            sh1, sc1, g1, sh2, sc2, g2 = mods
            q_bf, k, v, u, xc, z, ab = _inproj(x, sc1, sh1, norm_mix[l][None, :], w_in_bf[l], qg[l], kg[l], hm_sb,
                                               offs, tb)
            if grp == "p":
                k3 = k.reshape(bp, seq, w_sb)
                oa = _attn_prompt(q_bf.reshape(bp, seq, w_sb), jnp.swapaxes(k3, 1, 2).astype(BF16),
                                  v.reshape(bp, seq, w_sb).astype(BF16), sb_bias[l], tq, tk).reshape(n_p, w_sb)
                nb, steps, x0r, x0i = bp, seq, zero_ssm, zero_ssm
                ts, tp = ts_p, tp_p
            else:
                pad = ((0, 0), (0, 0), (0, page - dseq))
                ktn = jnp.pad(jnp.swapaxes(k.reshape(bs, dseq, w_sb), 1, 2), pad)
                vtn = jnp.pad(jnp.swapaxes(v.reshape(bs, dseq, w_sb), 1, 2), pad)
                oa = _attn_decode(q_bf.astype(F32).reshape(bs, dseq, w_sb), ktn, vtn, ck_t, cv_t, l, page_table,
                                  bias_rows[l], 8 if page_table.shape[1] % 8 == 0 else 1).reshape(n_s, w_sb)
                nb, steps = bs, dseq
                x0r, x0i = state_ssm_re[l].reshape(bs, ns), state_ssm_im[l].reshape(bs, ns)
                ts, tp = dseq, tp_s

            u_tm = u.reshape(nb, steps, w_ssm).swapaxes(0, 1).reshape(steps * nb, w_ssm)
            ob_tm, fr, fi = _s5(u_tm, x0r, x0i, a1r, a1i, wb, wc, d_skip, wglu_bf, nb, ts)
            ob = ob_tm.reshape(steps, nb, w_ssm).swapaxes(0, 1).reshape(steps * nb, w_ssm)
            ssm_re = fr[-nb:].reshape(nb, n_grp, n_state)
            ssm_im = fi[-nb:].reshape(nb, n_grp, n_state)

            if grp == "p":
                oc, gdn_s = _gdn(xc, None, zero_halo, z, ab, gdn_conv[l], alog[l], dtb[l], gng[l], hm_gd, ex_gd,
                                 zero_gdn, bp, rb_p, ch_p, 1)
                new_buf = xc.reshape(bp, seq, 3 * w_gd)[:, seq - (CONV_WIDTH - 1):]
            else:
                ext = jnp.concatenate([state_conv[l], xc.reshape(bs, dseq, 3 * w_gd)], axis=1)
                shifts = [ext[:, CONV_WIDTH - 1 - j:CONV_WIDTH - 1 - j + dseq].reshape(n_s, 3 * w_gd)
                          for j in (1, 2, 3)]
                oc, gdn_s = _gdn(xc, shifts, None, z, ab, gdn_conv[l], alog[l], dtb[l], gng[l], hm_gd, ex_gd,
                                 state_gdn[l], 1, n_s, ch_s, bs)
                new_buf = ext[:, dseq:]

            x = _out_ffn(x, oa, ob, oc, g1, sc2, sh2, g2, norm_ffn[l][None, :], w_out_bf[l], wg_bf[l], wu_bf[l],
                         wd_bf[l], tbf) if grp == "s" else _out_ffn(
                x, oa, ob, oc, modsf[2], modsf[4], modsf[3], modsf[5], norm_ffn[l][None, :], w_out_bf[l], wg_bf[l],
                wu_bf[l], wd_bf[l], tbf)
            bn = bp if grp == "p" else bs
            sq = seq if grp == "p" else dseq
            new = (k.reshape(bn, sq, n_sb, HEAD_DIM), v.reshape(bn, sq, n_sb, HEAD_DIM), new_buf, ssm_re, ssm_im,
                   gdn_s)
            tgt = st_p if grp == "p" else st_s
            for i_ in range(6):
                tgt[i_].append(new[i_])
            if grp == "p":
                xp = x
            else:
                xs = x

    outs_p = [jnp.stack(t) for t in st_p]
    outs_s = [jnp.stack(t) for t in st_s]
    return (xp.reshape(bp, seq, d), xs.reshape(bs, dseq, d), *outs_p, *outs_s)
```

```python
import functools
import math

import jax
import jax.numpy as jnp
from jax import lax
from jax.experimental import pallas as pl
from jax.experimental.pallas import tpu as pltpu

F32 = jnp.float32
BF16 = jnp.bfloat16
EPS = 1e-6
HEAD_DIM = 64
SSM_CH = 16
SSM_STATE = 64
CONV_WIDTH = 4
GDN_CHUNK = 64
N_ADA = 6
LANES = 128
SUBLANES = 8
VMEM_LIMIT = 56 * 1024 * 1024


def _cparams(*sem):
    return pltpu.CompilerParams(dimension_semantics=sem, vmem_limit_bytes=VMEM_LIMIT)


def _bdot(a, b):
    return jnp.dot(a.astype(BF16), b.astype(BF16), preferred_element_type=F32)


def _split(x):
    hi = x.astype(BF16)
    lo = (x - hi.astype(F32)).astype(BF16)
    return hi, lo


def _dot2(x, m):
    hi, lo = _split(x)
    return jnp.dot(hi, m, preferred_element_type=F32) + jnp.dot(lo, m, preferred_element_type=F32)


def _dot2_left(m, x):
    hi, lo = _split(x)
    return jnp.dot(m, hi, preferred_element_type=F32) + jnp.dot(m, lo, preferred_element_type=F32)


def _sigmoid(x):
    return 1.0 / (1.0 + jnp.exp(-x))


def _softplus(x):
    return jnp.maximum(x, 0.0) + jnp.log(1.0 + jnp.exp(-jnp.abs(x)))


def _full(shape):
    n = len(shape)
    return pl.BlockSpec(shape, lambda *_: (0,) * n)


def _resident(shape):
    n = len(shape)
    return pl.BlockSpec(shape, lambda *_: (0,) * n, pipeline_mode=pl.Buffered(1))


def _ada_kernel(c_ref, w_ref, b_ref, o_ref):
    c = c_ref[...]
    s = c * _sigmoid(c)
    o_ref[...] = _bdot(s, w_ref[...]) + b_ref[...]


def _ada(c_all, w_ada, b_ada):
    depth, d, n = w_ada.shape
    r = c_all.shape[0]
    tn = 1536 if n % 1536 == 0 else n
    return pl.pallas_call(
        _ada_kernel,
        grid=(depth, n // tn),
        in_specs=[
            pl.BlockSpec((r, d), lambda l, j: (0, 0)),
            pl.BlockSpec((None, d, tn), lambda l, j: (l, 0, j)),
            pl.BlockSpec((None, 1, tn), lambda l, j: (l, 0, j)),
        ],
        out_specs=pl.BlockSpec((None, r, tn), lambda l, j: (l, 0, j)),
        out_shape=jax.ShapeDtypeStruct((depth, r, n), F32),
        compiler_params=_cparams("parallel", "parallel"),
        name="ada_mod",
    )(c_all, w_ada, b_ada.reshape(depth, 1, n))


def _inproj_kernel(x_ref, sc_ref, sh_ref, g_ref, w_ref, qg_ref, kg_ref, hm_ref,
                   q_ref, k_ref, v_ref, u_ref, xc_ref, z_ref, ab_ref, *attn_refs, offs):
    x = x_ref[...]
    h = x * lax.rsqrt(jnp.mean(x * x, axis=-1, keepdims=True) + EPS) * g_ref[...]
    h = h * (1.0 + sc_ref[...]) + sh_ref[...]
    hb = h.astype(BF16)

    def proj(a, b):
        return jnp.dot(hb, w_ref[:, a:b], preferred_element_type=F32)

    hm = hm_ref[...]

    def headnorm(t, gain):
        ms = _bdot(t * t, hm) * (1.0 / HEAD_DIM)
        return t * lax.rsqrt(ms + EPS) * gain

    o_q, o_k, o_v, o_u, o_xc, o_z, o_ab, o_end = offs
    q_ref[...] = (headnorm(proj(o_q, o_k), qg_ref[...]) * (HEAD_DIM ** -0.5)).astype(BF16)
    k = headnorm(proj(o_k, o_v), kg_ref[...])
    v = proj(o_v, o_u)
    k_ref[...] = k
    v_ref[...] = v
    if attn_refs:
        kt_ref, vb_ref = attn_refs
        kt_ref[...] = k.T.astype(BF16)
        vb_ref[...] = v.astype(BF16)
    u_ref[...] = proj(o_u, o_xc)
    xc_ref[...] = proj(o_xc, o_z)
    z_ref[...] = proj(o_z, o_ab)
    ab_ref[...] = proj(o_ab, o_end)


def _inproj(x, sc, sh, gain, w_bf, qg, kg, hm, offs, tb, attn_seq=None):
    n, d = x.shape
    nb = n // tb
    per = nb // sc.shape[0]
    r = sc.shape[1]
    widths = [offs[i + 1] - offs[i] for i in range(7)]
    dts = [BF16] + [F32] * 6
    mod_spec = pl.BlockSpec((None, r, d), lambda i: (i // per, 0, 0))
    out_specs = [pl.BlockSpec((tb, w), lambda i: (i, 0)) for w in widths]
    out_shape = [jax.ShapeDtypeStruct((n, w), dt) for w, dt in zip(widths, dts)]
    if attn_seq is not None:
        bps = attn_seq // tb
        out_specs += [pl.BlockSpec((None, widths[1], tb), lambda i: (i // bps, 0, i % bps)),
                      pl.BlockSpec((tb, widths[2]), lambda i: (i, 0))]
        out_shape += [jax.ShapeDtypeStruct((n // attn_seq, widths[1], attn_seq), BF16),
                      jax.ShapeDtypeStruct((n, widths[2]), BF16)]
    return pl.pallas_call(
        functools.partial(_inproj_kernel, offs=offs),
        grid=(nb,),
        in_specs=[
            pl.BlockSpec((tb, d), lambda i: (i, 0)),
            mod_spec, mod_spec,
            _full((1, d)),
            _resident(w_bf.shape),
            _full(qg.shape), _full(kg.shape), _full(hm.shape),
        ],
        out_specs=out_specs,
        out_shape=out_shape,
        compiler_params=_cparams("parallel"),
        name="inproj",
    )(x, sc, sh, gain, w_bf, qg, kg, hm)


LOG2E = 1.4426950408889634


def _sb_softplus(z2, mask):
    sp2 = jnp.maximum(z2, 0.0) + jnp.log2(1.0 + jnp.exp2(-jnp.abs(z2)))
    return sp2 if mask is None else jnp.where(mask, sp2, 0.0)


def _sb_weights(z2, sp2, later2, mask):
    w = jnp.exp2(z2 - sp2 - later2)
    return w if mask is None else jnp.where(mask, w, 0.0)


def _attn_kernel(bias_ref, q_ref, kt_ref, v_ref, o_ref, *, tq, tk):
    p = pl.program_id(1)
    i = pl.program_id(2)
    nsub = tq // tk
    q2 = q_ref[...]
    lane = lax.broadcasted_iota(jnp.int32, q2.shape, 1)
    row = lax.broadcasted_iota(jnp.int32, (tk, tk), 0)
    col = lax.broadcasted_iota(jnp.int32, (tk, tk), 1)
    tri = (row > col).astype(BF16)
    diag = col < row
    zero = jnp.zeros_like(q2)
    qh = [jnp.where(lane < HEAD_DIM, q2, zero), jnp.where(lane >= HEAD_DIM, q2, zero)]
    bias = [bias_ref[2 * p] * LOG2E, bias_ref[2 * p + 1] * LOG2E]

    def tile(qs, j, carry, mask):
        start = pl.multiple_of(j * tk, tk)
        kt = kt_ref[:, pl.ds(start, tk)]
        vv = v_ref[pl.ds(start, tk), :]
        out = []
        for hh in range(2):
            acc, later = carry[hh]
            z = jnp.dot(qs[hh], kt, preferred_element_type=F32) * LOG2E + bias[hh]
            sp = _sb_softplus(z, mask)
            local = jnp.dot(sp.astype(BF16), tri, preferred_element_type=F32)
            w = _sb_weights(z, sp, local + later, mask)
            acc = acc + jnp.dot(w.astype(BF16), vv, preferred_element_type=F32)
            out.append((acc, later + jnp.sum(sp, axis=1, keepdims=True)))
        return tuple(out)

    subs = []
    for s in range(nsub):
        qs = [qh[hh][s * tk:(s + 1) * tk] for hh in range(2)]
        c = ((jnp.zeros((tk, LANES), F32), jnp.zeros((tk, 1), F32)),) * 2
        for jj in range(s, -1, -1):
            c = tile(qs, i * nsub + jj, c, diag if jj == s else None)
        subs.append(c)
    carry = tuple((jnp.concatenate([subs[s][hh][0] for s in range(nsub)], axis=0),
                   jnp.concatenate([subs[s][hh][1] for s in range(nsub)], axis=0)) for hh in range(2))
    carry = lax.fori_loop(0, i * nsub, lambda jj, c: tile(qh, i * nsub - 1 - jj, c, None), carry)
    o_ref[...] = jnp.where(lane < HEAD_DIM, carry[0][0], carry[1][0])


def _attn_prompt(q_bf, kt_bf, v_bf, bias, tq, tk):
    b, s, w = q_bf.shape
    return pl.pallas_call(
        functools.partial(_attn_kernel, tq=tq, tk=tk),
        grid=(b, w // LANES, s // tq),
        in_specs=[
            pl.BlockSpec(memory_space=pltpu.SMEM),
            pl.BlockSpec((None, tq, LANES), lambda bb, p, i: (bb, i, p)),
            pl.BlockSpec((None, LANES, s), lambda bb, p, i: (bb, p, 0)),
            pl.BlockSpec((None, s, LANES), lambda bb, p, i: (bb, 0, p)),
        ],
        out_specs=pl.BlockSpec((None, tq, LANES), lambda bb, p, i: (bb, i, p)),
        out_shape=jax.ShapeDtypeStruct((b, s, w), F32),
        compiler_params=_cparams("parallel", "parallel", "arbitrary"),
        name="sb_attn_prompt",
    )(bias, q_bf, kt_bf, v_bf)


def _attn_decode_kernel(pt_ref, bias_ref, q_ref, ktn_ref, vtn_ref, *rest, n_heads, dec_seq, ppb):
    k_refs = rest[:ppb]
    v_refs = rest[ppb:2 * ppb]
    o_ref = rest[2 * ppb]
    acc_ref, r_ref, qbd_ref = rest[2 * ppb + 1:]
    j = pl.program_id(1)
    rows = n_heads * dec_seq
    w_all = n_heads * HEAD_DIM
    slot = lax.broadcasted_iota(jnp.int32, (rows, LANES), 1)
    rid = lax.broadcasted_iota(jnp.int32, (rows, LANES), 0)
    trow = lax.broadcasted_iota(jnp.int32, (LANES, LANES), 0)
    tcol = lax.broadcasted_iota(jnp.int32, (LANES, LANES), 1)
    tri = (trow > tcol).astype(BF16)
    bias = bias_ref[...] * LOG2E

    def pages(kts, vts, mask, acc, later):
        qbd = qbd_ref[...]
        zs = [jnp.dot(qbd, kt.astype(BF16), preferred_element_type=F32) * LOG2E + bias for kt in kts]
        sps = [_sb_softplus(z, mask) for z in zs]
        loc_all = jnp.dot(jnp.concatenate(sps, axis=0).astype(BF16), tri, preferred_element_type=F32)
        for m, (z, sp, vt) in enumerate(zip(zs, sps, vts)):
            w = _sb_weights(z, sp, loc_all[m * rows:(m + 1) * rows] + later, mask)
            acc = acc + lax.dot_general(w.astype(BF16), vt.astype(BF16), (((1,), (1,)), ((), ())),
                                        preferred_element_type=F32)
            later = later + jnp.sum(sp, axis=1, keepdims=True)
        return acc, later

    @pl.when(j == 0)
    def _():
        q = q_ref[...]
        lq = lax.broadcasted_iota(jnp.int32, q.shape, 1)
        blocks = [jnp.where((lq >= h * HEAD_DIM) & (lq < (h + 1) * HEAD_DIM), q, 0.0) for h in range(n_heads)]
        qbd_ref[...] = jnp.concatenate(blocks, axis=0).astype(BF16)
        acc, later = pages([ktn_ref[...]], [vtn_ref[...]], slot < (rid & (dec_seq - 1)),
                           jnp.zeros(acc_ref.shape, F32), jnp.zeros(r_ref.shape, F32))
        acc_ref[...] = acc
        r_ref[...] = later

    acc, later = pages([k_refs[m][...].reshape(w_all, LANES) for m in range(ppb)],
                       [v_refs[m][...].reshape(w_all, LANES) for m in range(ppb)], None, acc_ref[...], r_ref[...])
    acc_ref[...] = acc
    r_ref[...] = later

    @pl.when(j == pl.num_programs(1) - 1)
    def _():
        acc = acc_ref[...]
        la = lax.broadcasted_iota(jnp.int32, (dec_seq, w_all), 1)
        out = jnp.zeros((dec_seq, w_all), F32)
        for h in range(n_heads):
            sel = (la >= h * HEAD_DIM) & (la < (h + 1) * HEAD_DIM)
            out = jnp.where(sel, acc[h * dec_seq:(h + 1) * dec_seq, :], out)
        o_ref[...] = out


def _pages_per_step(n_pages):
    return next(c for c in (16, 8, 4, 2, 1) if n_pages % c == 0)


def _attn_decode(q, ktn, vtn, ck_t, cv_t, layer, page_table, bias_rows, ppb):
    bs, t, w_all = q.shape
    n_heads = w_all // HEAD_DIM
    n_pages = page_table.shape[1]
    steps = n_pages // ppb
    rows = n_heads * t

    def page_spec(m):
        return pl.BlockSpec((None, None, n_heads, HEAD_DIM, LANES),
                            lambda b, j, pt, m=m: (layer, pt[b, n_pages - 1 - (j * ppb + m)], 0, 0, 0))

    grid_spec = pltpu.PrefetchScalarGridSpec(
        num_scalar_prefetch=1,
        grid=(bs, steps),
        in_specs=[
            pl.BlockSpec((rows, 1), lambda b, j, pt: (0, 0)),
            pl.BlockSpec((None, t, w_all), lambda b, j, pt: (b, 0, 0)),
            pl.BlockSpec((None, w_all, LANES), lambda b, j, pt: (b, 0, 0)),
            pl.BlockSpec((None, w_all, LANES), lambda b, j, pt: (b, 0, 0)),
        ] + [page_spec(m) for m in range(ppb)] + [page_spec(m) for m in range(ppb)],
        out_specs=pl.BlockSpec((None, t, w_all), lambda b, j, pt: (b, 0, 0)),
        scratch_shapes=[pltpu.VMEM((rows, w_all), F32), pltpu.VMEM((rows, 1), F32), pltpu.VMEM((rows, w_all), BF16)],
    )
    return pl.pallas_call(
        functools.partial(_attn_decode_kernel, n_heads=n_heads, dec_seq=t, ppb=ppb),
        grid_spec=grid_spec,
        out_shape=jax.ShapeDtypeStruct((bs, t, w_all), F32),
        compiler_params=_cparams("parallel", "arbitrary"),
        name="sb_attn_decode",
    )(page_table, bias_rows, q, ktn, vtn, *([ck_t] * ppb), *([cv_t] * ppb))


def _s5_disc_kernel(are_ref, aim_ref, ldt_ref, bre_ref, bim_ref, abr_ref, abi_ref, bbr_ref, bbi_ref):
    dt = jnp.exp(ldt_ref[...])
    lam_re = jnp.minimum(are_ref[...], -1e-4)
    lam_im = aim_ref[...]
    mag = jnp.exp(lam_re * dt)
    ab_re = mag * jnp.cos(lam_im * dt)
    ab_im = mag * jnp.sin(lam_im * dt)
    den = lam_re * lam_re + lam_im * lam_im
    nr = ab_re - 1.0
    f_re = (nr * lam_re + ab_im * lam_im) / den
    f_im = (ab_im * lam_re - nr * lam_im) / den
    abr_ref[...] = ab_re
    abi_ref[...] = ab_im
    br = bre_ref[...]
    bi = bim_ref[...]
    bbr_ref[...] = f_re * br - f_im * bi
    bbi_ref[...] = f_re * bi + f_im * br


def _s5_disc(a_re, a_im, log_dt, b_re_t, b_im_t):
    g, _, p = a_re.shape
    ch = b_re_t.shape[1]
    return pl.pallas_call(
        _s5_disc_kernel,
        out_shape=[jax.ShapeDtypeStruct((g, 1, p), F32)] * 2 + [jax.ShapeDtypeStruct((g, ch, p), F32)] * 2,
        name="s5_disc",
    )(a_re, a_im, log_dt, b_re_t, b_im_t)


def _cmul(ar, ai, xr, xi):
    return ar * xr - ai * xi, ar * xi + ai * xr


def _s5_kernel(u_ref, x0r_ref, x0i_ref, a1r_ref, a1i_ref, wb_ref, wc_ref, d_ref, wg_ref,
               y_ref, fr_ref, fi_ref, st_r, st_i, cr_ref, ci_ref, xs_ref, *, nb, tp):
    step = pl.program_id(0)
    tr = nb * tp
    rows = u_ref.shape[0]
    ns = a1r_ref.shape[1]

    @pl.when(step == 0)
    def _():
        pad = jnp.zeros((tr - nb, ns), F32)
        st_r[...] = jnp.concatenate([pad, x0r_ref[...]], axis=0) if tp > 1 else x0r_ref[...]
        st_i[...] = jnp.concatenate([pad, x0i_ref[...]], axis=0) if tp > 1 else x0i_ref[...]

    u = u_ref[...]
    bu = _bdot(u, wb_ref[...])
    bur, bui = bu[:, :ns], bu[:, ns:]
    a1r, a1i = a1r_ref[...], a1i_ref[...]
    rid = lax.broadcasted_iota(jnp.int32, (rows, ns), 0)
    if tp > 1:
        sr = jnp.where(rid < nb, 0.0, pltpu.roll(bur, nb, axis=0))
        si = jnp.where(rid < nb, 0.0, pltpu.roll(bui, nb, axis=0))
        pr, pi = _cmul(a1r, a1i, sr, si)
        cr = bur + pr
        ci = bui + pi
    else:
        cr, ci = bur, bui
    cr_ref[...] = cr
    ci_ref[...] = ci
    tid = lax.broadcasted_iota(jnp.int32, (tr, ns), 0)
    lr, li = st_r[...], st_i[...]
    if tp > 1:
        lr = jnp.where(tid < nb, pltpu.roll(lr, nb, axis=0), 0.0)
        li = jnp.where(tid < nb, pltpu.roll(li, nb, axis=0), 0.0)
    fr, fi = _cmul(a1r, a1i, lr, li)
    if tp > 1:
        gr, gi = _cmul(a1r, a1i, pltpu.roll(fr, nb, axis=0), pltpu.roll(fi, nb, axis=0))
        fr = jnp.where(tid < nb, fr, gr)
        fi = jnp.where(tid < nb, fi, gi)
    cr_ref[0:tr, :] = cr_ref[0:tr, :] + fr
    ci_ref[0:tr, :] = ci_ref[0:tr, :] + fi

    apr, api = _cmul(a1r, a1i, a1r, a1i) if tp > 1 else (a1r, a1i)

    def body(k, carry):
        xr, xi = carry
        off = pl.multiple_of(k * tr, tr)
        nr, ni = _cmul(apr, api, xr, xi)
        nr = nr + cr_ref[pl.ds(off, tr), :]
        ni = ni + ci_ref[pl.ds(off, tr), :]
        xs_ref[pl.ds(off, tr), 0:ns] = nr
        xs_ref[pl.ds(off, tr), ns:2 * ns] = ni
        return nr, ni

    xr, xi = lax.fori_loop(0, rows // tr, body, (jnp.zeros((tr, ns), F32), jnp.zeros((tr, ns), F32)))
    st_r[...] = xr
    st_i[...] = xi
    fr_ref[...] = xr
    fi_ref[...] = xi

    y = _bdot(xs_ref[...], wc_ref[...]) + d_ref[...] * u
    g = 0.5 * y * (1.0 + jnp.tanh(math.sqrt(2.0 / math.pi) * (y + 0.044715 * (y * y * y))))
    y_ref[...] = g * _sigmoid(_bdot(g, wg_ref[...]))


def _s5(u_tm, x0r, x0i, a1r, a1i, wb, wc, d_skip, w_glu, nb, ts):
    n, c = u_tm.shape
    ns = a1r.shape[1]
    tp = max(1, SUBLANES // nb)
    tr = nb * tp
    rows = ts * nb
    return pl.pallas_call(
        functools.partial(_s5_kernel, nb=nb, tp=tp),
        grid=(n // rows,),
        in_specs=[
            pl.BlockSpec((rows, c), lambda i: (i, 0)),
            _full((nb, ns)), _full((nb, ns)),
            _full((1, ns)), _full((1, ns)),
            _full(wb.shape), _full(wc.shape), _full((1, c)), _full(w_glu.shape),
        ],
        out_specs=[pl.BlockSpec((rows, c), lambda i: (i, 0)), _full((tr, ns)), _full((tr, ns))],
        out_shape=[jax.ShapeDtypeStruct((n, c), F32), jax.ShapeDtypeStruct((tr, ns), F32),
                   jax.ShapeDtypeStruct((tr, ns), F32)],
        scratch_shapes=[pltpu.VMEM((tr, ns), F32), pltpu.VMEM((tr, ns), F32),
                        pltpu.VMEM((rows, ns), F32), pltpu.VMEM((rows, ns), F32),
                        pltpu.VMEM((rows, 2 * ns), F32)],
        compiler_params=_cparams("arbitrary"),
        name="s5_scan",
    )(u_tm, x0r, x0i, a1r, a1i, wb, wc, d_skip, w_glu)


def _gdn_kernel(*refs, n_heads, ch, n_seq, carry_conv):
    if carry_conv:
        (xc_ref, halo_ref, z_ref, ab_ref, wconv_ref, alog_ref, dtb_ref, gn_ref, hm_ref, ex_ref, s0_ref,
         oc_ref, sfin_ref, s_scr, tail_scr, x_scr, ac_scr, qd_scr, kd_scr, egl_scr, o_scr, m_scr, t_scr) = refs
    else:
        (xc_ref, sh1_ref, sh2_ref, sh3_ref, z_ref, ab_ref, wconv_ref, alog_ref, dtb_ref, gn_ref, hm_ref, ex_ref,
         s0_ref, oc_ref, sfin_ref, s_scr, x_scr, ac_scr, qd_scr, kd_scr, egl_scr, o_scr, m_scr, t_scr) = refs
    step = pl.program_id(1)
    r = xc_ref.shape[0]
    w_all = n_heads * HEAD_DIM
    nc = r // (n_seq * ch)

    @pl.when(step == 0)
    def _():
        s_scr[...] = s0_ref[...]
        if carry_conv:
            tail_scr[...] = halo_ref[...]

    x = xc_ref[...]
    if carry_conv:
        ext = jnp.concatenate([tail_scr[...], x], axis=0)
        sh = [ext[SUBLANES - j:SUBLANES - j + r] for j in (1, 2, 3)]
        tail_scr[...] = x[r - SUBLANES:, :]
    else:
        sh = [sh1_ref[...], sh2_ref[...], sh3_ref[...]]
    wc = wconv_ref[...]
    conv = sh[2] * wc[0:1] + sh[1] * wc[1:2] + sh[0] * wc[2:3] + x * wc[3:4]
    conv = conv * _sigmoid(conv)
    q = conv[:, 0:w_all]
    k = conv[:, w_all:2 * w_all]
    v = conv[:, 2 * w_all:3 * w_all]
    hm = hm_ref[...]
    q = q * lax.rsqrt(_bdot(q * q, hm) + EPS) * (HEAD_DIM ** -0.5)
    k = k * lax.rsqrt(_bdot(k * k, hm) + EPS)

    ab = ab_ref[...]
    g = -jnp.exp(alog_ref[...]) * _softplus(ab + dtb_ref[...])
    beta = _sigmoid(ab)
    ri = lax.broadcasted_iota(jnp.int32, (r, r), 0)
    ci = lax.broadcasted_iota(jnp.int32, (r, r), 1)
    lg = int(math.log2(ch))
    same = (ri >> lg) == (ci >> lg)
    lower = same & (ci <= ri)
    strict = same & (ci < ri)
    g_cum = _dot2_left(lower.astype(BF16), g)
    g_tot = _dot2_left(same.astype(BF16), g)
    ex = ex_ref[...]
    gc_e = _dot2(g_cum, ex)
    gl_e = _dot2(g_tot, ex)
    beta_e = _dot2(pltpu.roll(beta, LANES - n_heads, axis=1), ex)
    g_t = g_cum.T

    e_gc = jnp.exp(gc_e)
    kb = k * beta_e
    xall = jnp.concatenate([v * beta_e, kb * e_gc], axis=1)
    qd_scr[...] = q * e_gc
    kd_scr[...] = k * jnp.exp(gl_e - gc_e)
    egl_scr[...] = jnp.exp(gl_e)
    selm = ((lax.broadcasted_iota(jnp.int32, (r, LANES), 0) & (ch - 1))
            == lax.broadcasted_iota(jnp.int32, (r, LANES), 1)).astype(BF16)

    def level_mask(ls):
        return (((ri >> (ls + 1)) == (ci >> (ls + 1))) & (((ri >> ls) & 1) == 1) & (((ci >> ls) & 1) == 0))

    eye = (ri == ci).astype(F32)
    heads = range(n_heads)
    hsl = [slice(h * HEAD_DIM, (h + 1) * HEAD_DIM) for h in heads]
    nt = (((1,), (1,)), ((), ()))
    khs = [k[:, hs].astype(BF16) for hs in hsl]
    kks = [lax.dot_general(kb[:, hs].astype(BF16), kh, nt, preferred_element_type=F32) for hs, kh in zip(hsl, khs)]
    qks = [lax.dot_general(q[:, hs].astype(BF16), kh, nt, preferred_element_type=F32) for hs, kh in zip(hsl, khs)]
    mask0 = level_mask(0)
    attns = []
    for h in heads:
        diff = g_cum[:, h:h + 1] - g_t[h:h + 1, :]
        decay = jnp.exp(jnp.where(lower, diff, 0.0))
        m = jnp.where(strict, kks[h] * decay, 0.0)
        m_scr[h] = m
        t_scr[h] = eye - jnp.where(mask0, m, 0.0)
        attns.append(jnp.where(lower, qks[h] * decay, 0.0).astype(BF16))
    for h in heads:
        ac_scr[h] = jnp.dot(attns[h], selm, preferred_element_type=F32)
    for ls in range(1, lg):
        mask = level_mask(ls)
        tbs = [t_scr[h].astype(BF16) for h in heads]
        ys = [jnp.dot(tbs[h], jnp.where(mask, m_scr[h], 0.0).astype(BF16), preferred_element_type=F32)
              for h in heads]
        for h in heads:
            t_scr[h] = t_scr[h] - jnp.dot(ys[h].astype(BF16), tbs[h], preferred_element_type=F32)
    for h in heads:
        xh = jnp.concatenate([xall[:, hsl[h]], xall[:, w_all + h * HEAD_DIM:w_all + (h + 1) * HEAD_DIM]], axis=1)
        x_scr[h] = _dot2(t_scr[h], xh.astype(BF16))

    def seq_body(s, _):
        sts = [s_scr[s, h] for h in heads]
        for c in range(nc):
            off = pl.multiple_of((s * nc + c) * ch, ch)
            rows = pl.ds(off, ch)
            for h in heads:
                hs = hsl[h]
                xh = x_scr[h, rows, :]
                stb = sts[h].astype(BF16)
                v_new = xh[:, 0:HEAD_DIM] - jnp.dot(xh[:, HEAD_DIM:].astype(BF16), stb, preferred_element_type=F32)
                vnb = v_new.astype(BF16)
                o = jnp.dot(qd_scr[rows, hs].astype(BF16), stb, preferred_element_type=F32)
                o = o + jnp.dot(ac_scr[h, rows, 0:ch].astype(BF16), vnb, preferred_element_type=F32)
                o_scr[rows, hs] = o
                sts[h] = sts[h] * egl_scr[pl.ds(off, 1), hs] + lax.dot_general(
                    kd_scr[rows, hs].astype(BF16), vnb, (((0,), (0,)), ((), ())), preferred_element_type=F32)
        for h in heads:
            s_scr[s, h] = sts[h]
        return 0

    lax.fori_loop(0, n_seq, seq_body, 0)

    o = o_scr[...]
    o = o * lax.rsqrt(_bdot(o * o, hm) * (1.0 / HEAD_DIM) + EPS) * gn_ref[...]
    zz = z_ref[...]
    oc_ref[...] = o * (zz * _sigmoid(zz))

    @pl.when(step == pl.num_programs(1) - 1)
    def _():
        sfin_ref[...] = s_scr[...]


def _gdn(xc, shifts, halo, z, ab, wconv, alog, dtb, gn, hm, ex, s0, n_batch, rb, ch, n_seq):
    n, w3 = xc.shape
    w_all = w3 // 3
    n_heads = w_all // HEAD_DIM
    per = n // n_batch // rb
    carry_conv = shifts is None
    row_spec = lambda w: pl.BlockSpec((rb, w), lambda b, i: (b * per + i, 0))
    st_spec = pl.BlockSpec((n_seq, n_heads, HEAD_DIM, HEAD_DIM), lambda b, i: (b, 0, 0, 0))
    ins = [xc]
    specs = [row_spec(w3)]
    if carry_conv:
        ins.append(halo)
        specs.append(pl.BlockSpec((None, SUBLANES, w3), lambda b, i: (b, 0, 0)))
    else:
        ins += list(shifts)
        specs += [row_spec(w3)] * 3
    ins += [z, ab, wconv, alog, dtb, gn, hm, ex, s0]
    specs += [row_spec(w_all), row_spec(LANES), _full(wconv.shape), _full((1, LANES)), _full((1, LANES)),
              _full((1, w_all)), _full(hm.shape), _full(ex.shape), st_spec]
    scratch = [pltpu.VMEM((n_seq, n_heads, HEAD_DIM, HEAD_DIM), F32)]
    if carry_conv:
        scratch.append(pltpu.VMEM((SUBLANES, w3), F32))
    scratch += [pltpu.VMEM((n_heads, rb, 2 * HEAD_DIM), F32), pltpu.VMEM((n_heads, rb, LANES), F32),
                pltpu.VMEM((rb, w_all), F32), pltpu.VMEM((rb, w_all), F32), pltpu.VMEM((rb, w_all), F32),
                pltpu.VMEM((rb, w_all), F32), pltpu.VMEM((n_heads, rb, rb), F32), pltpu.VMEM((n_heads, rb, rb), F32)]
    return pl.pallas_call(
        functools.partial(_gdn_kernel, n_heads=n_heads, ch=ch, n_seq=n_seq, carry_conv=carry_conv),
        grid=(n_batch, per),
        in_specs=specs,
        out_specs=[row_spec(w_all), st_spec],
        out_shape=[jax.ShapeDtypeStruct((n, w_all), F32),
                   jax.ShapeDtypeStruct((n_batch * n_seq, n_heads, HEAD_DIM, HEAD_DIM), F32)],
        scratch_shapes=scratch,
        compiler_params=_cparams("parallel", "arbitrary"),
        name="gdn",
    )(*ins)


def _out_ffn_kernel(x_ref, oa_ref, ob_ref, oc_ref, g1_ref, sc_ref, sh_ref, g2_ref, gain_ref,
                    wo_ref, wg_ref, wu_ref, wd_ref, y_ref, *, wa, wb, n_f):
    mix = _bdot(oa_ref[...], wo_ref[0:wa, :])
    mix = mix + _bdot(ob_ref[...], wo_ref[wa:wa + wb, :])
    mix = mix + _bdot(oc_ref[...], wo_ref[wa + wb:, :])
    x = x_ref[...] + g1_ref[...] * mix
    h = x * lax.rsqrt(jnp.mean(x * x, axis=-1, keepdims=True) + EPS) * gain_ref[...]
    h = h * (1.0 + sc_ref[...]) + sh_ref[...]
    hb = h.astype(BF16)
    f = wg_ref.shape[1]
    fc = f // n_f
    ff = jnp.zeros(x.shape, F32)
    for c in range(n_f):
        gt = jnp.dot(hb, wg_ref[:, c * fc:(c + 1) * fc], preferred_element_type=F32)
        up = jnp.dot(hb, wu_ref[:, c * fc:(c + 1) * fc], preferred_element_type=F32)
        act = (gt * _sigmoid(gt)) * up
        ff = ff + jnp.dot(act.astype(BF16), wd_ref[c * fc:(c + 1) * fc, :], preferred_element_type=F32)
    y_ref[...] = x + g2_ref[...] * ff


def _out_ffn(x, oa, ob, oc, g1, sc2, sh2, g2, gain, wo, wg, wu, wd, tb):
    n, d = x.shape
    nb = n // tb
    per = nb // g1.shape[0]
    r = g1.shape[1]
    f = wg.shape[1]
    n_f = 2 if (f // 2) % LANES == 0 else 1
    mod_spec = pl.BlockSpec((None, r, d), lambda i: (i // per, 0, 0))
    row = lambda w: pl.BlockSpec((tb, w), lambda i: (i, 0))
    return pl.pallas_call(
        functools.partial(_out_ffn_kernel, wa=oa.shape[1], wb=ob.shape[1], n_f=n_f),
        grid=(nb,),
        in_specs=[row(d), row(oa.shape[1]), row(ob.shape[1]), row(oc.shape[1]),
                  mod_spec, mod_spec, mod_spec, mod_spec, _full((1, d)),
                  _resident(wo.shape), _resident(wg.shape), _resident(wu.shape), _resident(wd.shape)],
        out_specs=row(d),
        out_shape=jax.ShapeDtypeStruct((n, d), F32),
        compiler_params=_cparams("parallel"),
        name="out_ffn",
    )(x, oa, ob, oc, g1, sc2, sh2, g2, gain, wo, wg, wu, wd)


def _block_diag(blocks):
    g, a, b = blocks.shape
    eye = jnp.eye(g, dtype=blocks.dtype)
    return (blocks[:, :, None, :] * eye[:, None, :, None]).reshape(g * a, g * b)


def _mods(mod, rows_per_seq, tb):
    d = mod.shape[1] // N_ADA
    parts = [mod[:, i * d:(i + 1) * d] for i in range(N_ADA)]
    if rows_per_seq >= tb:
        return [p[:, None, :] for p in parts]
    rep = [jnp.repeat(p, rows_per_seq, axis=0) for p in parts]
    return [p.reshape(-1, tb, d) for p in rep]


def kernel(x_prompt, x_sample, cache_k, cache_v, state_conv, state_ssm_re, state_ssm_im, state_gdn, page_table, c_prompt, c_sample, w_ada, b_ada, norm_mix, norm_ffn, w_in, sb_qnorm, sb_knorm, sb_bias, ssm_a_re, ssm_a_im, ssm_log_dt, ssm_b_re, ssm_b_im, ssm_c_re, ssm_c_im, ssm_d, ssm_w_glu, gdn_conv, gdn_a_log, gdn_dt_bias, gdn_norm, w_out, ffn_gate, ffn_up, ffn_down):
    bp, seq, d = x_prompt.shape
    bs, dseq, _ = x_sample.shape
    depth = w_ada.shape[0]
    n_sb = sb_bias.shape[1]
    n_gd = gdn_a_log.shape[1]
    n_grp, n_state = ssm_a_re.shape[1:]
    w_sb = n_sb * HEAD_DIM
    w_gd = n_gd * HEAD_DIM
    w_ssm = n_grp * SSM_CH
    sizes = (w_sb, w_sb, w_sb, w_ssm, 3 * w_gd, w_gd)
    offs = [0]
    for s_ in sizes:
        offs.append(offs[-1] + s_)
    offs.append(offs[-1] + LANES)
    offs = tuple(offs)
    in_width = w_in.shape[2]
    ns = n_grp * n_state
    n_p = bp * seq
    n_s = bs * dseq
    page = cache_k.shape[2]

    tb_p = min(512, seq)
    tb_f = min(512, seq)
    tq = min(512, seq)
    tk = min(256, seq)
    rb_p = min(256, seq)
    ch_p = math.gcd(seq, GDN_CHUNK)
    ch_s = math.gcd(dseq, GDN_CHUNK)
    ts_p = min(128, seq)

    w_in_bf = jnp.pad(w_in, ((0, 0), (0, 0), (0, offs[-1] - in_width))).astype(BF16)
    w_out_bf = w_out.astype(BF16)
    wg_bf, wu_bf, wd_bf = ffn_gate.astype(BF16), ffn_up.astype(BF16), ffn_down.astype(BF16)
    hm_sb = _block_diag(jnp.ones((n_sb, HEAD_DIM, HEAD_DIM), BF16))
    hm_gd = _block_diag(jnp.ones((n_gd, HEAD_DIM, HEAD_DIM), BF16))
    ex_gd = jnp.repeat(jnp.eye(LANES, n_gd, dtype=BF16), HEAD_DIM, axis=1)
    qg = jnp.tile(sb_qnorm, (1, n_sb))[:, None, :]
    kg = jnp.tile(sb_knorm, (1, n_sb))[:, None, :]
    gng = jnp.tile(gdn_norm, (1, n_gd))[:, None, :]
    alog = jnp.pad(gdn_a_log, ((0, 0), (0, LANES - n_gd)))[:, None, :]
    dtb = jnp.pad(gdn_dt_bias, ((0, 0), (0, LANES - n_gd)))[:, None, :]
    bias_rows = jnp.repeat(sb_bias, dseq, axis=1)[:, :, None]
    ck_t = jnp.transpose(cache_k, (0, 1, 3, 4, 2))
    cv_t = jnp.transpose(cache_v, (0, 1, 3, 4, 2))

    mod_all = _ada(jnp.concatenate([c_prompt, c_sample], axis=0), w_ada, b_ada)

    xp = x_prompt.reshape(n_p, d)
    xs = x_sample.reshape(n_s, d)
    st_p = [[] for _ in range(6)]
    st_s = [[] for _ in range(6)]
    zero_halo = jnp.zeros((bp, SUBLANES, 3 * w_gd), F32)
    zero_ssm = jnp.zeros((bp, ns), F32)
    zero_gdn = jnp.zeros((bp, n_gd, HEAD_DIM, HEAD_DIM), F32)
    tp_p = max(1, SUBLANES // bp)
    tp_s = max(1, SUBLANES // bs)

    for l in range(depth):
        abr, abi, bbr, bbi = _s5_disc(ssm_a_re[l][:, None, :], ssm_a_im[l][:, None, :], ssm_log_dt[l][:, None, None],
                                      jnp.transpose(ssm_b_re[l], (0, 2, 1)), jnp.transpose(ssm_b_im[l], (0, 2, 1)))
        a1r, a1i = abr.reshape(1, ns), abi.reshape(1, ns)
        wb = jnp.concatenate([_block_diag(bbr), _block_diag(bbi)], axis=1).astype(BF16)
        wc = jnp.concatenate([_block_diag(jnp.transpose(ssm_c_re[l], (0, 2, 1))),
                              -_block_diag(jnp.transpose(ssm_c_im[l], (0, 2, 1)))], axis=0).astype(BF16)
        d_skip = ssm_d[l][None, :]
        wglu_bf = ssm_w_glu[l].astype(BF16)

        mods_p = _mods(mod_all[l, :bp], seq, tb_p)
        mods_pf = _mods(mod_all[l, :bp], seq, tb_f)
        mods_s = _mods(mod_all[l, bp:], dseq, n_s)

        for grp in ("p", "s"):
            if grp == "p":
                x, mods, modsf, tb, tbf = xp, mods_p, mods_pf, tb_p, tb_f
            else:
                x, mods, modsf, tb, tbf = xs, mods_s, mods_s, n_s, n_s
            sh1, sc1, g1, sh2, sc2, g2 = mods
            q_bf, k, v, u, xc, z, ab, *attn_in = _inproj(x, sc1, sh1, norm_mix[l][None, :], w_in_bf[l], qg[l], kg[l],
                                                         hm_sb, offs, tb, seq if grp == "p" else None)
            if grp == "p":
                kt_bf, v_bf = attn_in
                oa = _attn_prompt(q_bf.reshape(bp, seq, w_sb), kt_bf, v_bf.reshape(bp, seq, w_sb), sb_bias[l],
                                  tq, tk).reshape(n_p, w_sb)
                nb, steps, x0r, x0i = bp, seq, zero_ssm, zero_ssm
                ts, tp = ts_p, tp_p
            else:
                pad = ((0, 0), (0, 0), (0, page - dseq))
                ktn = jnp.pad(jnp.swapaxes(k.reshape(bs, dseq, w_sb), 1, 2), pad)
                vtn = jnp.pad(jnp.swapaxes(v.reshape(bs, dseq, w_sb), 1, 2), pad)
                oa = _attn_decode(q_bf.astype(F32).reshape(bs, dseq, w_sb), ktn, vtn, ck_t, cv_t, l, page_table,
                                  bias_rows[l], _pages_per_step(page_table.shape[1])).reshape(n_s, w_sb)
                nb, steps = bs, dseq
                x0r, x0i = state_ssm_re[l].reshape(bs, ns), state_ssm_im[l].reshape(bs, ns)
                ts, tp = dseq, tp_s

            u_tm = u.reshape(nb, steps, w_ssm).swapaxes(0, 1).reshape(steps * nb, w_ssm)
            ob_tm, fr, fi = _s5(u_tm, x0r, x0i, a1r, a1i, wb, wc, d_skip, wglu_bf, nb, ts)
            ob = ob_tm.reshape(steps, nb, w_ssm).swapaxes(0, 1).reshape(steps * nb, w_ssm)
            ssm_re = fr[-nb:].reshape(nb, n_grp, n_state)
            ssm_im = fi[-nb:].reshape(nb, n_grp, n_state)

            if grp == "p":
                oc, gdn_s = _gdn(xc, None, zero_halo, z, ab, gdn_conv[l], alog[l], dtb[l], gng[l], hm_gd, ex_gd,
                                 zero_gdn, bp, rb_p, ch_p, 1)
                new_buf = xc.reshape(bp, seq, 3 * w_gd)[:, seq - (CONV_WIDTH - 1):]
            else:
                ext = jnp.concatenate([state_conv[l], xc.reshape(bs, dseq, 3 * w_gd)], axis=1)
                shifts = [ext[:, CONV_WIDTH - 1 - j:CONV_WIDTH - 1 - j + dseq].reshape(n_s, 3 * w_gd)
                          for j in (1, 2, 3)]
                oc, gdn_s = _gdn(xc, shifts, None, z, ab, gdn_conv[l], alog[l], dtb[l], gng[l], hm_gd, ex_gd,
                                 state_gdn[l], 1, n_s, ch_s, bs)
                new_buf = ext[:, dseq:]

            x = _out_ffn(x, oa, ob, oc, g1, sc2, sh2, g2, norm_ffn[l][None, :], w_out_bf[l], wg_bf[l], wu_bf[l],
                         wd_bf[l], tbf) if grp == "s" else _out_ffn(
                x, oa, ob, oc, modsf[2], modsf[4], modsf[3], modsf[5], norm_ffn[l][None, :], w_out_bf[l], wg_bf[l],
                wu_bf[l], wd_bf[l], tbf)
            bn = bp if grp == "p" else bs
            sq = seq if grp == "p" else dseq
            new = (k.reshape(bn, sq, n_sb, HEAD_DIM), v.reshape(bn, sq, n_sb, HEAD_DIM), new_buf, ssm_re, ssm_im,
                   gdn_s)
            tgt = st_p if grp == "p" else st_s
            for i_ in range(6):
                tgt[i_].append(new[i_])
            if grp == "p":
                xp = x
            else:
                xs = x

    outs_p = [jnp.stack(t) for t in st_p]
    outs_s = [jnp.stack(t) for t in st_s]
    return (xp.reshape(bp, seq, d), xs.reshape(bs, dseq, d), *outs_p, *outs_s)
```

```python
import functools
import math

import jax
import jax.numpy as jnp
from jax import lax
from jax.experimental import pallas as pl
from jax.experimental.pallas import tpu as pltpu

F32 = jnp.float32
BF16 = jnp.bfloat16
EPS = 1e-6
HEAD_DIM = 64
SSM_CH = 16
SSM_STATE = 64
CONV_WIDTH = 4
GDN_CHUNK = 64
N_ADA = 6
LANES = 128
SUBLANES = 8
VMEM_LIMIT = 56 * 1024 * 1024


def _cparams(*sem):
    return pltpu.CompilerParams(dimension_semantics=sem, vmem_limit_bytes=VMEM_LIMIT)


def _bdot(a, b):
    return jnp.dot(a.astype(BF16), b.astype(BF16), preferred_element_type=F32)


def _split(x):
    hi = x.astype(BF16)
    lo = (x - hi.astype(F32)).astype(BF16)
    return hi, lo


def _dot2(x, m):
    hi, lo = _split(x)
    return jnp.dot(hi, m, preferred_element_type=F32) + jnp.dot(lo, m, preferred_element_type=F32)


def _dot2_left(m, x):
    hi, lo = _split(x)
    return jnp.dot(m, hi, preferred_element_type=F32) + jnp.dot(m, lo, preferred_element_type=F32)


def _sigmoid(x):
    return 1.0 / (1.0 + jnp.exp(-x))


def _softplus(x):
    return jnp.maximum(x, 0.0) + jnp.log(1.0 + jnp.exp(-jnp.abs(x)))


def _full(shape):
    n = len(shape)
    return pl.BlockSpec(shape, lambda *_: (0,) * n)


def _resident(shape):
    n = len(shape)
    return pl.BlockSpec(shape, lambda *_: (0,) * n, pipeline_mode=pl.Buffered(1))


def _ada_kernel(c_ref, w_ref, b_ref, o_ref):
    c = c_ref[...]
    s = c * _sigmoid(c)
    o_ref[...] = _bdot(s, w_ref[...]) + b_ref[...]


def _ada(c_all, w_ada, b_ada):
    depth, d, n = w_ada.shape
    r = c_all.shape[0]
    tn = 1536 if n % 1536 == 0 else n
    return pl.pallas_call(
        _ada_kernel,
        grid=(depth, n // tn),
        in_specs=[
            pl.BlockSpec((r, d), lambda l, j: (0, 0)),
            pl.BlockSpec((None, d, tn), lambda l, j: (l, 0, j)),
            pl.BlockSpec((None, 1, tn), lambda l, j: (l, 0, j)),
        ],
        out_specs=pl.BlockSpec((None, r, tn), lambda l, j: (l, 0, j)),
        out_shape=jax.ShapeDtypeStruct((depth, r, n), F32),
        compiler_params=_cparams("parallel", "parallel"),
        name="ada_mod",
    )(c_all, w_ada, b_ada.reshape(depth, 1, n))


def _inproj_kernel(x_ref, sc_ref, sh_ref, g_ref, w_ref, qg_ref, kg_ref, hm_ref,
                   q_ref, k_ref, v_ref, u_ref, xc_ref, z_ref, ab_ref, *attn_refs, offs):
    x = x_ref[...]
    h = x * lax.rsqrt(jnp.mean(x * x, axis=-1, keepdims=True) + EPS) * g_ref[...]
    h = h * (1.0 + sc_ref[...]) + sh_ref[...]
    hb = h.astype(BF16)

    def proj(a, b):
        return jnp.dot(hb, w_ref[:, a:b], preferred_element_type=F32)

    hm = hm_ref[...]

    def headnorm(t, gain):
        ms = _bdot(t * t, hm) * (1.0 / HEAD_DIM)
        return t * lax.rsqrt(ms + EPS) * gain

    o_q, o_k, o_v, o_u, o_xc, o_z, o_ab, o_end = offs
    q_ref[...] = (headnorm(proj(o_q, o_k), qg_ref[...]) * (HEAD_DIM ** -0.5)).astype(BF16)
    k = headnorm(proj(o_k, o_v), kg_ref[...])
    v = proj(o_v, o_u)
    k_ref[...] = k
    v_ref[...] = v
    if attn_refs:
        kt_ref, vb_ref = attn_refs
        kt_ref[...] = k.T.astype(BF16)
        vb_ref[...] = v.astype(BF16)
    u_ref[...] = proj(o_u, o_xc)
    xc_ref[...] = proj(o_xc, o_z)
    z_ref[...] = proj(o_z, o_ab)
    ab_ref[...] = proj(o_ab, o_end)


def _inproj(x, sc, sh, gain, w_bf, qg, kg, hm, offs, tb, attn_seq=None):
    n, d = x.shape
    nb = n // tb
    per = nb // sc.shape[0]
    r = sc.shape[1]
    widths = [offs[i + 1] - offs[i] for i in range(7)]
    dts = [BF16] + [F32] * 6
    mod_spec = pl.BlockSpec((None, r, d), lambda i: (i // per, 0, 0))
    out_specs = [pl.BlockSpec((tb, w), lambda i: (i, 0)) for w in widths]
    out_shape = [jax.ShapeDtypeStruct((n, w), dt) for w, dt in zip(widths, dts)]
    if attn_seq is not None:
        bps = attn_seq // tb
        out_specs += [pl.BlockSpec((None, widths[1], tb), lambda i: (i // bps, 0, i % bps)),
                      pl.BlockSpec((tb, widths[2]), lambda i: (i, 0))]
        out_shape += [jax.ShapeDtypeStruct((n // attn_seq, widths[1], attn_seq), BF16),
                      jax.ShapeDtypeStruct((n, widths[2]), BF16)]
    return pl.pallas_call(
        functools.partial(_inproj_kernel, offs=offs),
        grid=(nb,),
        in_specs=[
            pl.BlockSpec((tb, d), lambda i: (i, 0)),
            mod_spec, mod_spec,
            _full((1, d)),
            _resident(w_bf.shape),
            _full(qg.shape), _full(kg.shape), _full(hm.shape),
        ],
        out_specs=out_specs,
        out_shape=out_shape,
        compiler_params=_cparams("parallel"),
        name="inproj",
    )(x, sc, sh, gain, w_bf, qg, kg, hm)


LOG2E = 1.4426950408889634


def _sb_softplus(z2, mask):
    sp2 = jnp.maximum(z2, 0.0) + jnp.log2(1.0 + jnp.exp2(-jnp.abs(z2)))
    return sp2 if mask is None else jnp.where(mask, sp2, 0.0)


def _sb_weights(z2, sp2, later2, mask):
    w = jnp.exp2(z2 - sp2 - later2)
    return w if mask is None else jnp.where(mask, w, 0.0)


def _attn_kernel(bias_ref, q_ref, kt_ref, v_ref, o_ref, *, tq, tk):
    p = pl.program_id(1)
    i = pl.program_id(2)
    nsub = tq // tk
    q2 = q_ref[...]
    lane = lax.broadcasted_iota(jnp.int32, q2.shape, 1)
    row = lax.broadcasted_iota(jnp.int32, (tk, tk), 0)
    col = lax.broadcasted_iota(jnp.int32, (tk, tk), 1)
    tri = (row > col).astype(BF16)
    diag = col < row
    zero = jnp.zeros_like(q2)
    qh = [jnp.where(lane < HEAD_DIM, q2, zero), jnp.where(lane >= HEAD_DIM, q2, zero)]
    bias = [bias_ref[2 * p] * LOG2E, bias_ref[2 * p + 1] * LOG2E]

    def tile(qs, j, carry, mask):
        start = pl.multiple_of(j * tk, tk)
        kt = kt_ref[:, pl.ds(start, tk)]
        vv = v_ref[pl.ds(start, tk), :]
        out = []
        for hh in range(2):
            acc, later = carry[hh]
            z = jnp.dot(qs[hh], kt, preferred_element_type=F32) * LOG2E + bias[hh]
            sp = _sb_softplus(z, mask)
            local = jnp.dot(sp.astype(BF16), tri, preferred_element_type=F32)
            w = _sb_weights(z, sp, local + later, mask)
            acc = acc + jnp.dot(w.astype(BF16), vv, preferred_element_type=F32)
            out.append((acc, later + jnp.sum(sp, axis=1, keepdims=True)))
        return tuple(out)

    subs = []
    for s in range(nsub):
        qs = [qh[hh][s * tk:(s + 1) * tk] for hh in range(2)]
        c = ((jnp.zeros((tk, LANES), F32), jnp.zeros((tk, 1), F32)),) * 2
        for jj in range(s, -1, -1):
            c = tile(qs, i * nsub + jj, c, diag if jj == s else None)
        subs.append(c)
    carry = tuple((jnp.concatenate([subs[s][hh][0] for s in range(nsub)], axis=0),
                   jnp.concatenate([subs[s][hh][1] for s in range(nsub)], axis=0)) for hh in range(2))
    carry = lax.fori_loop(0, i * nsub, lambda jj, c: tile(qh, i * nsub - 1 - jj, c, None), carry)
    o_ref[...] = jnp.where(lane < HEAD_DIM, carry[0][0], carry[1][0])


def _attn_prompt(q_bf, kt_bf, v_bf, bias, tq, tk):
    b, s, w = q_bf.shape
    return pl.pallas_call(
        functools.partial(_attn_kernel, tq=tq, tk=tk),
        grid=(b, w // LANES, s // tq),
        in_specs=[
            pl.BlockSpec(memory_space=pltpu.SMEM),
            pl.BlockSpec((None, tq, LANES), lambda bb, p, i: (bb, i, p)),
            pl.BlockSpec((None, LANES, s), lambda bb, p, i: (bb, p, 0)),
            pl.BlockSpec((None, s, LANES), lambda bb, p, i: (bb, 0, p)),
        ],
        out_specs=pl.BlockSpec((None, tq, LANES), lambda bb, p, i: (bb, i, p)),
        out_shape=jax.ShapeDtypeStruct((b, s, w), F32),
        compiler_params=_cparams("parallel", "parallel", "arbitrary"),
        name="sb_attn_prompt",
    )(bias, q_bf, kt_bf, v_bf)


def _attn_decode_kernel(pt_ref, bias_ref, q_ref, ktn_ref, vtn_ref, *rest, n_heads, dec_seq, ppb):
    k_refs = rest[:ppb]
    v_refs = rest[ppb:2 * ppb]
    o_ref = rest[2 * ppb]
    acc_ref, r_ref, qbd_ref = rest[2 * ppb + 1:]
    j = pl.program_id(1)
    rows = n_heads * dec_seq
    w_all = n_heads * HEAD_DIM
    slot = lax.broadcasted_iota(jnp.int32, (rows, LANES), 1)
    rid = lax.broadcasted_iota(jnp.int32, (rows, LANES), 0)
    trow = lax.broadcasted_iota(jnp.int32, (LANES, LANES), 0)
    tcol = lax.broadcasted_iota(jnp.int32, (LANES, LANES), 1)
    tri = (trow > tcol).astype(BF16)
    bias = bias_ref[...] * LOG2E

    def pages(kts, vts, mask, acc, later):
        qbd = qbd_ref[...]
        zs = [jnp.dot(qbd, kt.astype(BF16), preferred_element_type=F32) * LOG2E + bias for kt in kts]
        sps = [_sb_softplus(z, mask) for z in zs]
        loc_all = jnp.dot(jnp.concatenate(sps, axis=0).astype(BF16), tri, preferred_element_type=F32)
        for m, (z, sp, vt) in enumerate(zip(zs, sps, vts)):
            w = _sb_weights(z, sp, loc_all[m * rows:(m + 1) * rows] + later, mask)
            acc = acc + lax.dot_general(w.astype(BF16), vt.astype(BF16), (((1,), (1,)), ((), ())),
                                        preferred_element_type=F32)
            later = later + jnp.sum(sp, axis=1, keepdims=True)
        return acc, later

    @pl.when(j == 0)
    def _():
        q = q_ref[...]
        lq = lax.broadcasted_iota(jnp.int32, q.shape, 1)
        blocks = [jnp.where((lq >= h * HEAD_DIM) & (lq < (h + 1) * HEAD_DIM), q, 0.0) for h in range(n_heads)]
        qbd_ref[...] = jnp.concatenate(blocks, axis=0).astype(BF16)
        acc, later = pages([ktn_ref[...]], [vtn_ref[...]], slot < (rid & (dec_seq - 1)),
                           jnp.zeros(acc_ref.shape, F32), jnp.zeros(r_ref.shape, F32))
        acc_ref[...] = acc
        r_ref[...] = later

    acc, later = pages([k_refs[m][...].reshape(w_all, LANES) for m in range(ppb)],
                       [v_refs[m][...].reshape(w_all, LANES) for m in range(ppb)], None, acc_ref[...], r_ref[...])
    acc_ref[...] = acc
    r_ref[...] = later

    @pl.when(j == pl.num_programs(1) - 1)
    def _():
        acc = acc_ref[...]
        la = lax.broadcasted_iota(jnp.int32, (dec_seq, w_all), 1)
        out = jnp.zeros((dec_seq, w_all), F32)
        for h in range(n_heads):
            sel = (la >= h * HEAD_DIM) & (la < (h + 1) * HEAD_DIM)
            out = jnp.where(sel, acc[h * dec_seq:(h + 1) * dec_seq, :], out)
        o_ref[...] = out


def _pages_per_step(n_pages):
    return next(c for c in (32, 16, 8, 4, 2, 1) if n_pages % c == 0)


def _attn_decode(q, ktn, vtn, ck_t, cv_t, layer, page_table, bias_rows, ppb):
    bs, t, w_all = q.shape
    n_heads = w_all // HEAD_DIM
    n_pages = page_table.shape[1]
    steps = n_pages // ppb
    rows = n_heads * t

    def page_spec(m):
        return pl.BlockSpec((None, None, n_heads, HEAD_DIM, LANES),
                            lambda b, j, pt, m=m: (layer, pt[b, n_pages - 1 - (j * ppb + m)], 0, 0, 0))

    grid_spec = pltpu.PrefetchScalarGridSpec(
        num_scalar_prefetch=1,
        grid=(bs, steps),
        in_specs=[
            pl.BlockSpec((rows, 1), lambda b, j, pt: (0, 0)),
            pl.BlockSpec((None, t, w_all), lambda b, j, pt: (b, 0, 0)),
            pl.BlockSpec((None, w_all, LANES), lambda b, j, pt: (b, 0, 0)),
            pl.BlockSpec((None, w_all, LANES), lambda b, j, pt: (b, 0, 0)),
        ] + [page_spec(m) for m in range(ppb)] + [page_spec(m) for m in range(ppb)],
        out_specs=pl.BlockSpec((None, t, w_all), lambda b, j, pt: (b, 0, 0)),
        scratch_shapes=[pltpu.VMEM((rows, w_all), F32), pltpu.VMEM((rows, 1), F32), pltpu.VMEM((rows, w_all), BF16)],
    )
    return pl.pallas_call(
        functools.partial(_attn_decode_kernel, n_heads=n_heads, dec_seq=t, ppb=ppb),
        grid_spec=grid_spec,
        out_shape=jax.ShapeDtypeStruct((bs, t, w_all), F32),
        compiler_params=_cparams("parallel", "arbitrary"),
        name="sb_attn_decode",
    )(page_table, bias_rows, q, ktn, vtn, *([ck_t] * ppb), *([cv_t] * ppb))


def _s5_disc_kernel(are_ref, aim_ref, ldt_ref, bre_ref, bim_ref, abr_ref, abi_ref, bbr_ref, bbi_ref):
    dt = jnp.exp(ldt_ref[...])
    lam_re = jnp.minimum(are_ref[...], -1e-4)
    lam_im = aim_ref[...]
    mag = jnp.exp(lam_re * dt)
    ab_re = mag * jnp.cos(lam_im * dt)
    ab_im = mag * jnp.sin(lam_im * dt)
    den = lam_re * lam_re + lam_im * lam_im
    nr = ab_re - 1.0
    f_re = (nr * lam_re + ab_im * lam_im) / den
    f_im = (ab_im * lam_re - nr * lam_im) / den
    abr_ref[...] = ab_re
    abi_ref[...] = ab_im
    br = bre_ref[...]
    bi = bim_ref[...]
    bbr_ref[...] = f_re * br - f_im * bi
    bbi_ref[...] = f_re * bi + f_im * br


def _s5_disc(a_re, a_im, log_dt, b_re_t, b_im_t):
    g, _, p = a_re.shape
    ch = b_re_t.shape[1]
    return pl.pallas_call(
        _s5_disc_kernel,
        out_shape=[jax.ShapeDtypeStruct((g, 1, p), F32)] * 2 + [jax.ShapeDtypeStruct((g, ch, p), F32)] * 2,
        name="s5_disc",
    )(a_re, a_im, log_dt, b_re_t, b_im_t)


def _cmul(ar, ai, xr, xi):
    return ar * xr - ai * xi, ar * xi + ai * xr


def _split3(x):
    hi = x.astype(BF16)
    r1 = x - hi.astype(F32)
    mid = r1.astype(BF16)
    return hi, mid, (r1 - mid.astype(F32)).astype(BF16)


def _s5_kernel(u_ref, perm_ref, permt_ref, x0r_ref, x0i_ref, a1r_ref, a1i_ref, wb_ref, wc_ref, d_ref, wg_ref,
               y_ref, fr_ref, fi_ref, st_r, st_i, cr_ref, ci_ref, xs_ref, *, nb, tp):
    step = pl.program_id(0)
    tr = nb * tp
    ts, c = u_ref.shape[1:]
    rows = nb * ts
    ns = a1r_ref.shape[1]

    @pl.when(step == 0)
    def _():
        pad = jnp.zeros((tr - nb, ns), F32)
        st_r[...] = jnp.concatenate([pad, x0r_ref[...]], axis=0) if tp > 1 else x0r_ref[...]
        st_i[...] = jnp.concatenate([pad, x0i_ref[...]], axis=0) if tp > 1 else x0i_ref[...]

    u_bt = u_ref[...].reshape(rows, c)
    u_tm = jnp.dot(perm_ref[...], u_bt.astype(BF16), preferred_element_type=F32)
    bu = _bdot(u_tm, wb_ref[...])
    bur, bui = bu[:, :ns], bu[:, ns:]
    a1r, a1i = a1r_ref[...], a1i_ref[...]
    rid = lax.broadcasted_iota(jnp.int32, (rows, ns), 0)
    if tp > 1:
        sr = jnp.where(rid < nb, 0.0, pltpu.roll(bur, nb, axis=0))
        si = jnp.where(rid < nb, 0.0, pltpu.roll(bui, nb, axis=0))
        pr, pi = _cmul(a1r, a1i, sr, si)
        cr = bur + pr
        ci = bui + pi
    else:
        cr, ci = bur, bui
    cr_ref[...] = cr
    ci_ref[...] = ci
    tid = lax.broadcasted_iota(jnp.int32, (tr, ns), 0)
    lr, li = st_r[...], st_i[...]
    if tp > 1:
        lr = jnp.where(tid < nb, pltpu.roll(lr, nb, axis=0), 0.0)
        li = jnp.where(tid < nb, pltpu.roll(li, nb, axis=0), 0.0)
    fr, fi = _cmul(a1r, a1i, lr, li)
    if tp > 1:
        gr, gi = _cmul(a1r, a1i, pltpu.roll(fr, nb, axis=0), pltpu.roll(fi, nb, axis=0))
        fr = jnp.where(tid < nb, fr, gr)
        fi = jnp.where(tid < nb, fi, gi)
    cr_ref[0:tr, :] = cr_ref[0:tr, :] + fr
    ci_ref[0:tr, :] = ci_ref[0:tr, :] + fi

    apr, api = _cmul(a1r, a1i, a1r, a1i) if tp > 1 else (a1r, a1i)
    apr = jnp.broadcast_to(apr, (tr, ns))
    api = jnp.broadcast_to(api, (tr, ns))

    def body(k, carry):
        xr, xi = carry
        off = pl.multiple_of(k * tr, tr)
        nr, ni = _cmul(apr, api, xr, xi)
        nr = nr + cr_ref[pl.ds(off, tr), :]
        ni = ni + ci_ref[pl.ds(off, tr), :]
        xs_ref[pl.ds(off, tr), 0:ns] = nr
        xs_ref[pl.ds(off, tr), ns:2 * ns] = ni
        return nr, ni

    xr, xi = lax.fori_loop(0, rows // tr, body, (jnp.zeros((tr, ns), F32), jnp.zeros((tr, ns), F32)))
    st_r[...] = xr
    st_i[...] = xi
    fr_ref[...] = xr
    fi_ref[...] = xi

    y_tm = _bdot(xs_ref[...], wc_ref[...])
    permt = permt_ref[...]
    y = sum(jnp.dot(permt, part, preferred_element_type=F32) for part in _split3(y_tm)) + d_ref[...] * u_bt
    g = 0.5 * y * (1.0 + jnp.tanh(math.sqrt(2.0 / math.pi) * (y + 0.044715 * (y * y * y))))
    y_ref[...] = (g * _sigmoid(_bdot(g, wg_ref[...]))).reshape(nb, ts, c)


def _s5(u3, x0r, x0i, a1r, a1i, wb, wc, d_skip, w_glu, ts):
    nb, s, c = u3.shape
    ns = a1r.shape[1]
    tp = max(1, SUBLANES // nb)
    tr = nb * tp
    rows = ts * nb
    dest = jnp.arange(rows)
    src = (dest % nb) * ts + dest // nb
    perm = (src[:, None] == dest[None, :]).astype(BF16)
    blk = pl.BlockSpec((nb, ts, c), lambda i: (0, i, 0))
    return pl.pallas_call(
        functools.partial(_s5_kernel, nb=nb, tp=tp),
        grid=(s // ts,),
        in_specs=[
            blk, _full((rows, rows)), _full((rows, rows)),
            _full((nb, ns)), _full((nb, ns)),
            _full((1, ns)), _full((1, ns)),
            _full(wb.shape), _full(wc.shape), _full((1, c)), _full(w_glu.shape),
        ],
        out_specs=[blk, _full((tr, ns)), _full((tr, ns))],
        out_shape=[jax.ShapeDtypeStruct((nb, s, c), F32), jax.ShapeDtypeStruct((tr, ns), F32),
                   jax.ShapeDtypeStruct((tr, ns), F32)],
        scratch_shapes=[pltpu.VMEM((tr, ns), F32), pltpu.VMEM((tr, ns), F32),
                        pltpu.VMEM((rows, ns), F32), pltpu.VMEM((rows, ns), F32),
                        pltpu.VMEM((rows, 2 * ns), F32)],
        compiler_params=_cparams("arbitrary"),
        name="s5_scan",
    )(u3, perm, perm.T, x0r, x0i, a1r, a1i, wb, wc, d_skip, w_glu)


def _gdn_kernel(*refs, n_heads, ch, n_seq, carry_conv):
    if carry_conv:
        (xc_ref, halo_ref, z_ref, ab_ref, wconv_ref, alog_ref, dtb_ref, gn_ref, hm_ref, ex_ref, s0_ref,
         oc_ref, sfin_ref, s_scr, tail_scr, x_scr, ac_scr, qd_scr, kd_scr, egl_scr, o_scr, m_scr, t_scr) = refs
    else:
        (xc_ref, sh1_ref, sh2_ref, sh3_ref, z_ref, ab_ref, wconv_ref, alog_ref, dtb_ref, gn_ref, hm_ref, ex_ref,
         s0_ref, oc_ref, sfin_ref, s_scr, x_scr, ac_scr, qd_scr, kd_scr, egl_scr, o_scr, m_scr, t_scr) = refs
    step = pl.program_id(1)
    r = xc_ref.shape[0]
    w_all = n_heads * HEAD_DIM
    nc = r // (n_seq * ch)

    @pl.when(step == 0)
    def _():
        s_scr[...] = s0_ref[...]
        if carry_conv:
            tail_scr[...] = halo_ref[...]

    x = xc_ref[...]
    if carry_conv:
        ext = jnp.concatenate([tail_scr[...], x], axis=0)
        sh = [ext[SUBLANES - j:SUBLANES - j + r] for j in (1, 2, 3)]
        tail_scr[...] = x[r - SUBLANES:, :]
    else:
        sh = [sh1_ref[...], sh2_ref[...], sh3_ref[...]]
    wc = wconv_ref[...]
    conv = sh[2] * wc[0:1] + sh[1] * wc[1:2] + sh[0] * wc[2:3] + x * wc[3:4]
    conv = conv * _sigmoid(conv)
    q = conv[:, 0:w_all]
    k = conv[:, w_all:2 * w_all]
    v = conv[:, 2 * w_all:3 * w_all]
    hm = hm_ref[...]
    q = q * lax.rsqrt(_bdot(q * q, hm) + EPS) * (HEAD_DIM ** -0.5)
    k = k * lax.rsqrt(_bdot(k * k, hm) + EPS)

    ab = ab_ref[...]
    g = -jnp.exp(alog_ref[...]) * _softplus(ab + dtb_ref[...])
    beta = _sigmoid(ab)
    ri = lax.broadcasted_iota(jnp.int32, (r, r), 0)
    ci = lax.broadcasted_iota(jnp.int32, (r, r), 1)
    lg = int(math.log2(ch))
    same = (ri >> lg) == (ci >> lg)
    lower = same & (ci <= ri)
    strict = same & (ci < ri)
    g_cum = _dot2_left(lower.astype(BF16), g)
    g_tot = _dot2_left(same.astype(BF16), g)
    ex = ex_ref[...]
    gc_e = _dot2(g_cum, ex)
    gl_e = _dot2(g_tot, ex)
    beta_e = _dot2(pltpu.roll(beta, LANES - n_heads, axis=1), ex)
    g_t = g_cum.T

    e_gc = jnp.exp(gc_e)
    kb = k * beta_e
    xall = jnp.concatenate([v * beta_e, kb * e_gc], axis=1)
    qd_scr[...] = q * e_gc
    kd_scr[...] = k * jnp.exp(gl_e - gc_e)
    egl_scr[...] = jnp.exp(gl_e)
    selm = ((lax.broadcasted_iota(jnp.int32, (r, LANES), 0) & (ch - 1))
            == lax.broadcasted_iota(jnp.int32, (r, LANES), 1)).astype(BF16)

    def level_mask(ls):
        return (((ri >> (ls + 1)) == (ci >> (ls + 1))) & (((ri >> ls) & 1) == 1) & (((ci >> ls) & 1) == 0))

    eye = (ri == ci).astype(F32)
    heads = range(n_heads)
    hsl = [slice(h * HEAD_DIM, (h + 1) * HEAD_DIM) for h in heads]
    nt = (((1,), (1,)), ((), ()))
    khs = [k[:, hs].astype(BF16) for hs in hsl]
    kks = [lax.dot_general(kb[:, hs].astype(BF16), kh, nt, preferred_element_type=F32) for hs, kh in zip(hsl, khs)]
    qks = [lax.dot_general(q[:, hs].astype(BF16), kh, nt, preferred_element_type=F32) for hs, kh in zip(hsl, khs)]
    mask0 = level_mask(0)
    attns = []
    for h in heads:
        diff = g_cum[:, h:h + 1] - g_t[h:h + 1, :]
        decay = jnp.exp(jnp.where(lower, diff, 0.0))
        m = jnp.where(strict, kks[h] * decay, 0.0)
        m_scr[h] = m
        t_scr[h] = eye - jnp.where(mask0, m, 0.0)
        attns.append(jnp.where(lower, qks[h] * decay, 0.0).astype(BF16))
    for h in heads:
        ac_scr[h] = jnp.dot(attns[h], selm, preferred_element_type=F32)
    for ls in range(1, lg):
        mask = level_mask(ls)
        tbs = [t_scr[h].astype(BF16) for h in heads]
        ys = [jnp.dot(tbs[h], jnp.where(mask, m_scr[h], 0.0).astype(BF16), preferred_element_type=F32)
              for h in heads]
        for h in heads:
            t_scr[h] = t_scr[h] - jnp.dot(ys[h].astype(BF16), tbs[h], preferred_element_type=F32)
    for h in heads:
        xh = jnp.concatenate([xall[:, hsl[h]], xall[:, w_all + h * HEAD_DIM:w_all + (h + 1) * HEAD_DIM]], axis=1)
        x_scr[h] = _dot2(t_scr[h], xh.astype(BF16))

    def seq_body(s, _):
        sts = [s_scr[s, h] for h in heads]
        for c in range(nc):
            off = pl.multiple_of((s * nc + c) * ch, ch)
            rows = pl.ds(off, ch)
            for h in heads:
                hs = hsl[h]
                xh = x_scr[h, rows, :]
                stb = sts[h].astype(BF16)
                v_new = xh[:, 0:HEAD_DIM] - jnp.dot(xh[:, HEAD_DIM:].astype(BF16), stb, preferred_element_type=F32)
                vnb = v_new.astype(BF16)
                o = jnp.dot(qd_scr[rows, hs].astype(BF16), stb, preferred_element_type=F32)
                o = o + jnp.dot(ac_scr[h, rows, 0:ch].astype(BF16), vnb, preferred_element_type=F32)
                o_scr[rows, hs] = o
                sts[h] = sts[h] * egl_scr[pl.ds(off, 1), hs] + lax.dot_general(
                    kd_scr[rows, hs].astype(BF16), vnb, (((0,), (0,)), ((), ())), preferred_element_type=F32)
        for h in heads:
            s_scr[s, h] = sts[h]
        return 0

    lax.fori_loop(0, n_seq, seq_body, 0)

    o = o_scr[...]
    o = o * lax.rsqrt(_bdot(o * o, hm) * (1.0 / HEAD_DIM) + EPS) * gn_ref[...]
    zz = z_ref[...]
    oc_ref[...] = o * (zz * _sigmoid(zz))

    @pl.when(step == pl.num_programs(1) - 1)
    def _():
        sfin_ref[...] = s_scr[...]


def _gdn(xc, shifts, halo, z, ab, wconv, alog, dtb, gn, hm, ex, s0, n_batch, rb, ch, n_seq):
    n, w3 = xc.shape
    w_all = w3 // 3
    n_heads = w_all // HEAD_DIM
    per = n // n_batch // rb
    carry_conv = shifts is None
    row_spec = lambda w: pl.BlockSpec((rb, w), lambda b, i: (b * per + i, 0))
    st_spec = pl.BlockSpec((n_seq, n_heads, HEAD_DIM, HEAD_DIM), lambda b, i: (b, 0, 0, 0))
    ins = [xc]
    specs = [row_spec(w3)]
    if carry_conv:
        ins.append(halo)
        specs.append(pl.BlockSpec((None, SUBLANES, w3), lambda b, i: (b, 0, 0)))
    else:
        ins += list(shifts)
        specs += [row_spec(w3)] * 3
    ins += [z, ab, wconv, alog, dtb, gn, hm, ex, s0]
    specs += [row_spec(w_all), row_spec(LANES), _full(wconv.shape), _full((1, LANES)), _full((1, LANES)),
              _full((1, w_all)), _full(hm.shape), _full(ex.shape), st_spec]
    scratch = [pltpu.VMEM((n_seq, n_heads, HEAD_DIM, HEAD_DIM), F32)]
    if carry_conv:
        scratch.append(pltpu.VMEM((SUBLANES, w3), F32))
    scratch += [pltpu.VMEM((n_heads, rb, 2 * HEAD_DIM), F32), pltpu.VMEM((n_heads, rb, LANES), F32),
                pltpu.VMEM((rb, w_all), F32), pltpu.VMEM((rb, w_all), F32), pltpu.VMEM((rb, w_all), F32),
                pltpu.VMEM((rb, w_all), F32), pltpu.VMEM((n_heads, rb, rb), F32), pltpu.VMEM((n_heads, rb, rb), F32)]
    return pl.pallas_call(
        functools.partial(_gdn_kernel, n_heads=n_heads, ch=ch, n_seq=n_seq, carry_conv=carry_conv),
        grid=(n_batch, per),
        in_specs=specs,
        out_specs=[row_spec(w_all), st_spec],
        out_shape=[jax.ShapeDtypeStruct((n, w_all), F32),
                   jax.ShapeDtypeStruct((n_batch * n_seq, n_heads, HEAD_DIM, HEAD_DIM), F32)],
        scratch_shapes=scratch,
        compiler_params=_cparams("parallel", "arbitrary"),
        name="gdn",
    )(*ins)


def _out_ffn_kernel(x_ref, oa_ref, ob_ref, oc_ref, g1_ref, sc_ref, sh_ref, g2_ref, gain_ref,
                    wo_ref, wg_ref, wu_ref, wd_ref, y_ref, *, wa, wb, n_f):
    mix = _bdot(oa_ref[...], wo_ref[0:wa, :])
    mix = mix + _bdot(ob_ref[...], wo_ref[wa:wa + wb, :])
    mix = mix + _bdot(oc_ref[...], wo_ref[wa + wb:, :])
    x = x_ref[...] + g1_ref[...] * mix
    h = x * lax.rsqrt(jnp.mean(x * x, axis=-1, keepdims=True) + EPS) * gain_ref[...]
    h = h * (1.0 + sc_ref[...]) + sh_ref[...]
    hb = h.astype(BF16)
    f = wg_ref.shape[1]
    fc = f // n_f
    ff = jnp.zeros(x.shape, F32)
    for c in range(n_f):
        gt = jnp.dot(hb, wg_ref[:, c * fc:(c + 1) * fc], preferred_element_type=F32)
        up = jnp.dot(hb, wu_ref[:, c * fc:(c + 1) * fc], preferred_element_type=F32)
        act = (gt * _sigmoid(gt)) * up
        ff = ff + jnp.dot(act.astype(BF16), wd_ref[c * fc:(c + 1) * fc, :], preferred_element_type=F32)
    y_ref[...] = x + g2_ref[...] * ff


def _out_ffn(x, oa, ob, oc, g1, sc2, sh2, g2, gain, wo, wg, wu, wd, tb):
    n, d = x.shape
    nb = n // tb
    per = nb // g1.shape[0]
    r = g1.shape[1]
    f = wg.shape[1]
    n_f = 2 if (f // 2) % LANES == 0 else 1
    mod_spec = pl.BlockSpec((None, r, d), lambda i: (i // per, 0, 0))
    row = lambda w: pl.BlockSpec((tb, w), lambda i: (i, 0))
    return pl.pallas_call(
        functools.partial(_out_ffn_kernel, wa=oa.shape[1], wb=ob.shape[1], n_f=n_f),
        grid=(nb,),
        in_specs=[row(d), row(oa.shape[1]), row(ob.shape[1]), row(oc.shape[1]),
                  mod_spec, mod_spec, mod_spec, mod_spec, _full((1, d)),
                  _resident(wo.shape), _resident(wg.shape), _resident(wu.shape), _resident(wd.shape)],
        out_specs=row(d),
        out_shape=jax.ShapeDtypeStruct((n, d), F32),
        compiler_params=_cparams("parallel"),
        name="out_ffn",
    )(x, oa, ob, oc, g1, sc2, sh2, g2, gain, wo, wg, wu, wd)


def _block_diag(blocks):
    g, a, b = blocks.shape
    eye = jnp.eye(g, dtype=blocks.dtype)
    return (blocks[:, :, None, :] * eye[:, None, :, None]).reshape(g * a, g * b)


def _mods(mod, rows_per_seq, tb):
    d = mod.shape[1] // N_ADA
    parts = [mod[:, i * d:(i + 1) * d] for i in range(N_ADA)]
    if rows_per_seq >= tb:
        return [p[:, None, :] for p in parts]
    rep = [jnp.repeat(p, rows_per_seq, axis=0) for p in parts]
    return [p.reshape(-1, tb, d) for p in rep]


def kernel(x_prompt, x_sample, cache_k, cache_v, state_conv, state_ssm_re, state_ssm_im, state_gdn, page_table, c_prompt, c_sample, w_ada, b_ada, norm_mix, norm_ffn, w_in, sb_qnorm, sb_knorm, sb_bias, ssm_a_re, ssm_a_im, ssm_log_dt, ssm_b_re, ssm_b_im, ssm_c_re, ssm_c_im, ssm_d, ssm_w_glu, gdn_conv, gdn_a_log, gdn_dt_bias, gdn_norm, w_out, ffn_gate, ffn_up, ffn_down):
    bp, seq, d = x_prompt.shape
    bs, dseq, _ = x_sample.shape
    depth = w_ada.shape[0]
    n_sb = sb_bias.shape[1]
    n_gd = gdn_a_log.shape[1]
    n_grp, n_state = ssm_a_re.shape[1:]
    w_sb = n_sb * HEAD_DIM
    w_gd = n_gd * HEAD_DIM
    w_ssm = n_grp * SSM_CH
    sizes = (w_sb, w_sb, w_sb, w_ssm, 3 * w_gd, w_gd)
    offs = [0]
    for s_ in sizes:
        offs.append(offs[-1] + s_)
    offs.append(offs[-1] + LANES)
    offs = tuple(offs)
    in_width = w_in.shape[2]
    ns = n_grp * n_state
    n_p = bp * seq
    n_s = bs * dseq
    page = cache_k.shape[2]

    tb_p = min(512, seq)
    tb_f = min(512, seq)
    tq = min(512, seq)
    tk = min(256, seq)
    rb_p = min(256, seq)
    ch_p = math.gcd(seq, GDN_CHUNK)
    ch_s = math.gcd(dseq, GDN_CHUNK)
    ts_p = min(128, seq)

    w_in_bf = jnp.pad(w_in, ((0, 0), (0, 0), (0, offs[-1] - in_width))).astype(BF16)
    w_out_bf = w_out.astype(BF16)
    wg_bf, wu_bf, wd_bf = ffn_gate.astype(BF16), ffn_up.astype(BF16), ffn_down.astype(BF16)
    hm_sb = _block_diag(jnp.ones((n_sb, HEAD_DIM, HEAD_DIM), BF16))
    hm_gd = _block_diag(jnp.ones((n_gd, HEAD_DIM, HEAD_DIM), BF16))
    ex_gd = jnp.repeat(jnp.eye(LANES, n_gd, dtype=BF16), HEAD_DIM, axis=1)
    qg = jnp.tile(sb_qnorm, (1, n_sb))[:, None, :]
    kg = jnp.tile(sb_knorm, (1, n_sb))[:, None, :]
    gng = jnp.tile(gdn_norm, (1, n_gd))[:, None, :]
    alog = jnp.pad(gdn_a_log, ((0, 0), (0, LANES - n_gd)))[:, None, :]
    dtb = jnp.pad(gdn_dt_bias, ((0, 0), (0, LANES - n_gd)))[:, None, :]
    bias_rows = jnp.repeat(sb_bias, dseq, axis=1)[:, :, None]
    ck_t = jnp.transpose(cache_k, (0, 1, 3, 4, 2))
    cv_t = jnp.transpose(cache_v, (0, 1, 3, 4, 2))

    mod_all = _ada(jnp.concatenate([c_prompt, c_sample], axis=0), w_ada, b_ada)

    xp = x_prompt.reshape(n_p, d)
    xs = x_sample.reshape(n_s, d)
    st_p = [[] for _ in range(6)]
    st_s = [[] for _ in range(6)]
    zero_halo = jnp.zeros((bp, SUBLANES, 3 * w_gd), F32)
    zero_ssm = jnp.zeros((bp, ns), F32)
    zero_gdn = jnp.zeros((bp, n_gd, HEAD_DIM, HEAD_DIM), F32)
    tp_p = max(1, SUBLANES // bp)
    tp_s = max(1, SUBLANES // bs)

    for l in range(depth):
        abr, abi, bbr, bbi = _s5_disc(ssm_a_re[l][:, None, :], ssm_a_im[l][:, None, :], ssm_log_dt[l][:, None, None],
                                      jnp.transpose(ssm_b_re[l], (0, 2, 1)), jnp.transpose(ssm_b_im[l], (0, 2, 1)))
        a1r, a1i = abr.reshape(1, ns), abi.reshape(1, ns)
        wb = jnp.concatenate([_block_diag(bbr), _block_diag(bbi)], axis=1).astype(BF16)
        wc = jnp.concatenate([_block_diag(jnp.transpose(ssm_c_re[l], (0, 2, 1))),
                              -_block_diag(jnp.transpose(ssm_c_im[l], (0, 2, 1)))], axis=0).astype(BF16)
        d_skip = ssm_d[l][None, :]
        wglu_bf = ssm_w_glu[l].astype(BF16)

        mods_p = _mods(mod_all[l, :bp], seq, tb_p)
        mods_pf = _mods(mod_all[l, :bp], seq, tb_f)
        mods_s = _mods(mod_all[l, bp:], dseq, n_s)

        for grp in ("p", "s"):
            if grp == "p":
                x, mods, modsf, tb, tbf = xp, mods_p, mods_pf, tb_p, tb_f
            else:
                x, mods, modsf, tb, tbf = xs, mods_s, mods_s, n_s, n_s
            sh1, sc1, g1, sh2, sc2, g2 = mods
            q_bf, k, v, u, xc, z, ab, *attn_in = _inproj(x, sc1, sh1, norm_mix[l][None, :], w_in_bf[l], qg[l], kg[l],
                                                         hm_sb, offs, tb, seq if grp == "p" else None)
            if grp == "p":
                kt_bf, v_bf = attn_in
                oa = _attn_prompt(q_bf.reshape(bp, seq, w_sb), kt_bf, v_bf.reshape(bp, seq, w_sb), sb_bias[l],
                                  tq, tk).reshape(n_p, w_sb)
                nb, steps, x0r, x0i = bp, seq, zero_ssm, zero_ssm
                ts, tp = ts_p, tp_p
            else:
                pad = ((0, 0), (0, 0), (0, page - dseq))
                ktn = jnp.pad(jnp.swapaxes(k.reshape(bs, dseq, w_sb), 1, 2), pad)
                vtn = jnp.pad(jnp.swapaxes(v.reshape(bs, dseq, w_sb), 1, 2), pad)
                oa = _attn_decode(q_bf.astype(F32).reshape(bs, dseq, w_sb), ktn, vtn, ck_t, cv_t, l, page_table,
                                  bias_rows[l], _pages_per_step(page_table.shape[1])).reshape(n_s, w_sb)
                nb, steps = bs, dseq
                x0r, x0i = state_ssm_re[l].reshape(bs, ns), state_ssm_im[l].reshape(bs, ns)
                ts, tp = dseq, tp_s

            ob3, fr, fi = _s5(u.reshape(nb, steps, w_ssm), x0r, x0i, a1r, a1i, wb, wc, d_skip, wglu_bf, ts)
            ob = ob3.reshape(steps * nb, w_ssm)
            ssm_re = fr[-nb:].reshape(nb, n_grp, n_state)
            ssm_im = fi[-nb:].reshape(nb, n_grp, n_state)

            if grp == "p":
                oc, gdn_s = _gdn(xc, None, zero_halo, z, ab, gdn_conv[l], alog[l], dtb[l], gng[l], hm_gd, ex_gd,
                                 zero_gdn, bp, rb_p, ch_p, 1)
                new_buf = xc.reshape(bp, seq, 3 * w_gd)[:, seq - (CONV_WIDTH - 1):]
            else:
                ext = jnp.concatenate([state_conv[l], xc.reshape(bs, dseq, 3 * w_gd)], axis=1)
                shifts = [ext[:, CONV_WIDTH - 1 - j:CONV_WIDTH - 1 - j + dseq].reshape(n_s, 3 * w_gd)
                          for j in (1, 2, 3)]
                oc, gdn_s = _gdn(xc, shifts, None, z, ab, gdn_conv[l], alog[l], dtb[l], gng[l], hm_gd, ex_gd,
                                 state_gdn[l], 1, n_s, ch_s, bs)
                new_buf = ext[:, dseq:]

            x = _out_ffn(x, oa, ob, oc, g1, sc2, sh2, g2, norm_ffn[l][None, :], w_out_bf[l], wg_bf[l], wu_bf[l],
                         wd_bf[l], tbf) if grp == "s" else _out_ffn(
                x, oa, ob, oc, modsf[2], modsf[4], modsf[3], modsf[5], norm_ffn[l][None, :], w_out_bf[l], wg_bf[l],
                wu_bf[l], wd_bf[l], tbf)
            bn = bp if grp == "p" else bs
            sq = seq if grp == "p" else dseq
            new = (k.reshape(bn, sq, n_sb, HEAD_DIM), v.reshape(bn, sq, n_sb, HEAD_DIM), new_buf, ssm_re, ssm_im,
                   gdn_s)
            tgt = st_p if grp == "p" else st_s
            for i_ in range(6):
                tgt[i_].append(new[i_])
            if grp == "p":
                xp = x
            else:
                xs = x

    outs_p = [jnp.stack(t) for t in st_p]
    outs_s = [jnp.stack(t) for t in st_s]
    return (xp.reshape(bp, seq, d), xs.reshape(bs, dseq, d), *outs_p, *outs_s)
```

```python
import functools
import math

import jax
import jax.numpy as jnp
from jax import lax
from jax.experimental import pallas as pl
from jax.experimental.pallas import tpu as pltpu

F32 = jnp.float32
BF16 = jnp.bfloat16
EPS = 1e-6
HEAD_DIM = 64
SSM_CH = 16
SSM_STATE = 64
CONV_WIDTH = 4
GDN_CHUNK = 64
N_ADA = 6
LANES = 128
SUBLANES = 8
VMEM_LIMIT = 56 * 1024 * 1024


def _cparams(*sem):
    return pltpu.CompilerParams(dimension_semantics=sem, vmem_limit_bytes=VMEM_LIMIT)


def _bdot(a, b):
    return jnp.dot(a.astype(BF16), b.astype(BF16), preferred_element_type=F32)


def _split(x):
    hi = x.astype(BF16)
    lo = (x - hi.astype(F32)).astype(BF16)
    return hi, lo


def _dot2(x, m):
    hi, lo = _split(x)
    return jnp.dot(hi, m, preferred_element_type=F32) + jnp.dot(lo, m, preferred_element_type=F32)


def _dot2_left(m, x):
    hi, lo = _split(x)
    return jnp.dot(m, hi, preferred_element_type=F32) + jnp.dot(m, lo, preferred_element_type=F32)


def _sigmoid(x):
    return 1.0 / (1.0 + jnp.exp(-x))


def _softplus(x):
    return jnp.maximum(x, 0.0) + jnp.log(1.0 + jnp.exp(-jnp.abs(x)))


def _full(shape):
    n = len(shape)
    return pl.BlockSpec(shape, lambda *_: (0,) * n)


def _resident(shape):
    n = len(shape)
    return pl.BlockSpec(shape, lambda *_: (0,) * n, pipeline_mode=pl.Buffered(1))


def _ada_kernel(c_ref, w_ref, b_ref, o_ref):
    c = c_ref[...]
    s = c * _sigmoid(c)
    o_ref[...] = _bdot(s, w_ref[...]) + b_ref[...]


def _ada(c_all, w_ada, b_ada):
    depth, d, n = w_ada.shape
    r = c_all.shape[0]
    tn = 1536 if n % 1536 == 0 else n
    return pl.pallas_call(
        _ada_kernel,
        grid=(depth, n // tn),
        in_specs=[
            pl.BlockSpec((r, d), lambda l, j: (0, 0)),
            pl.BlockSpec((None, d, tn), lambda l, j: (l, 0, j)),
            pl.BlockSpec((None, 1, tn), lambda l, j: (l, 0, j)),
        ],
        out_specs=pl.BlockSpec((None, r, tn), lambda l, j: (l, 0, j)),
        out_shape=jax.ShapeDtypeStruct((depth, r, n), F32),
        compiler_params=_cparams("parallel", "parallel"),
        name="ada_mod",
    )(c_all, w_ada, b_ada.reshape(depth, 1, n))


def _inproj_kernel(x_ref, sc_ref, sh_ref, g_ref, w_ref, qg_ref, kg_ref, hm_ref,
                   q_ref, k_ref, v_ref, u_ref, xc_ref, z_ref, ab_ref, *attn_refs, offs):
    x = x_ref[...]
    h = x * lax.rsqrt(jnp.mean(x * x, axis=-1, keepdims=True) + EPS) * g_ref[...]
    h = h * (1.0 + sc_ref[...]) + sh_ref[...]
    hb = h.astype(BF16)

    def proj(a, b):
        return jnp.dot(hb, w_ref[:, a:b], preferred_element_type=F32)

    hm = hm_ref[...]

    def headnorm(t, gain):
        ms = _bdot(t * t, hm) * (1.0 / HEAD_DIM)
        return t * lax.rsqrt(ms + EPS) * gain

    o_q, o_k, o_v, o_u, o_xc, o_z, o_ab, o_end = offs
    q_ref[...] = (headnorm(proj(o_q, o_k), qg_ref[...]) * (HEAD_DIM ** -0.5)).astype(BF16)
    k = headnorm(proj(o_k, o_v), kg_ref[...])
    v = proj(o_v, o_u)
    if attn_refs:
        kt_ref, vb_ref = attn_refs
        kt = k.T
        k_ref[...] = kt
        v_ref[...] = v.T
        kt_ref[...] = kt.astype(BF16)
        vb_ref[...] = v.astype(BF16)
    else:
        k_ref[...] = k
        v_ref[...] = v
    u_ref[...] = proj(o_u, o_xc)
    xc_ref[...] = proj(o_xc, o_z)
    z_ref[...] = proj(o_z, o_ab)
    ab_ref[...] = proj(o_ab, o_end)


def _inproj(x, sc, sh, gain, w_bf, qg, kg, hm, offs, tb, attn_seq=None):
    n, d = x.shape
    nb = n // tb
    per = nb // sc.shape[0]
    r = sc.shape[1]
    widths = [offs[i + 1] - offs[i] for i in range(7)]
    dts = [BF16] + [F32] * 6
    mod_spec = pl.BlockSpec((None, r, d), lambda i: (i // per, 0, 0))
    out_specs = [pl.BlockSpec((tb, w), lambda i: (i, 0)) for w in widths]
    out_shape = [jax.ShapeDtypeStruct((n, w), dt) for w, dt in zip(widths, dts)]
    if attn_seq is not None:
        bps = attn_seq // tb
        t_spec = pl.BlockSpec((None, widths[1], tb), lambda i: (i // bps, 0, i % bps))
        t_shape = (n // attn_seq, widths[1], attn_seq)
        out_specs[1:3] = [t_spec, t_spec]
        out_shape[1:3] = [jax.ShapeDtypeStruct(t_shape, F32)] * 2
        out_specs += [t_spec, pl.BlockSpec((tb, widths[2]), lambda i: (i, 0))]
        out_shape += [jax.ShapeDtypeStruct(t_shape, BF16), jax.ShapeDtypeStruct((n, widths[2]), BF16)]
    return pl.pallas_call(
        functools.partial(_inproj_kernel, offs=offs),
        grid=(nb,),
        in_specs=[
            pl.BlockSpec((tb, d), lambda i: (i, 0)),
            mod_spec, mod_spec,
            _full((1, d)),
            _resident(w_bf.shape),
            _full(qg.shape), _full(kg.shape), _full(hm.shape),
        ],
        out_specs=out_specs,
        out_shape=out_shape,
        compiler_params=_cparams("parallel"),
        name="inproj",
    )(x, sc, sh, gain, w_bf, qg, kg, hm)


LOG2E = 1.4426950408889634


def _sb_softplus(z2, mask):
    sp2 = jnp.maximum(z2, 0.0) + jnp.log2(1.0 + jnp.exp2(-jnp.abs(z2)))
    return sp2 if mask is None else jnp.where(mask, sp2, 0.0)


def _sb_weights(z2, sp2, later2, mask):
    w = jnp.exp2(z2 - sp2 - later2)
    return w if mask is None else jnp.where(mask, w, 0.0)


def _attn_kernel(bias_ref, q_ref, kt_ref, v_ref, o_ref, *, tq, tk):
    p = pl.program_id(1)
    i = pl.program_id(2)
    nsub = tq // tk
    q2 = q_ref[...]
    lane = lax.broadcasted_iota(jnp.int32, q2.shape, 1)
    row = lax.broadcasted_iota(jnp.int32, (tk, tk), 0)
    col = lax.broadcasted_iota(jnp.int32, (tk, tk), 1)
    tri = (row > col).astype(BF16)
    diag = col < row
    zero = jnp.zeros_like(q2)
    qh = [jnp.where(lane < HEAD_DIM, q2, zero), jnp.where(lane >= HEAD_DIM, q2, zero)]
    bias = [bias_ref[2 * p] * LOG2E, bias_ref[2 * p + 1] * LOG2E]

    def tile(qs, j, carry, mask):
        start = pl.multiple_of(j * tk, tk)
        kt = kt_ref[:, pl.ds(start, tk)]
        vv = v_ref[pl.ds(start, tk), :]
        out = []
        for hh in range(2):
            acc, later = carry[hh]
            z = jnp.dot(qs[hh], kt, preferred_element_type=F32) * LOG2E + bias[hh]
            sp = _sb_softplus(z, mask)
            local = jnp.dot(sp.astype(BF16), tri, preferred_element_type=F32)
            w = _sb_weights(z, sp, local + later, mask)
            acc = acc + jnp.dot(w.astype(BF16), vv, preferred_element_type=F32)
            out.append((acc, later + jnp.sum(sp, axis=1, keepdims=True)))
        return tuple(out)

    subs = []
    for s in range(nsub):
        qs = [qh[hh][s * tk:(s + 1) * tk] for hh in range(2)]
        c = ((jnp.zeros((tk, LANES), F32), jnp.zeros((tk, 1), F32)),) * 2
        for jj in range(s, -1, -1):
            c = tile(qs, i * nsub + jj, c, diag if jj == s else None)
        subs.append(c)
    carry = tuple((jnp.concatenate([subs[s][hh][0] for s in range(nsub)], axis=0),
                   jnp.concatenate([subs[s][hh][1] for s in range(nsub)], axis=0)) for hh in range(2))
    carry = lax.fori_loop(0, i * nsub, lambda jj, c: tile(qh, i * nsub - 1 - jj, c, None), carry)
    o_ref[...] = jnp.where(lane < HEAD_DIM, carry[0][0], carry[1][0])


def _attn_prompt(q_bf, kt_bf, v_bf, bias, tq, tk):
    b, s, w = q_bf.shape
    return pl.pallas_call(
        functools.partial(_attn_kernel, tq=tq, tk=tk),
        grid=(b, w // LANES, s // tq),
        in_specs=[
            pl.BlockSpec(memory_space=pltpu.SMEM),
            pl.BlockSpec((None, tq, LANES), lambda bb, p, i: (bb, i, p)),
            pl.BlockSpec((None, LANES, s), lambda bb, p, i: (bb, p, 0)),
            pl.BlockSpec((None, s, LANES), lambda bb, p, i: (bb, 0, p)),
        ],
        out_specs=pl.BlockSpec((None, tq, LANES), lambda bb, p, i: (bb, i, p)),
        out_shape=jax.ShapeDtypeStruct((b, s, w), F32),
        compiler_params=_cparams("parallel", "parallel", "arbitrary"),
        name="sb_attn_prompt",
    )(bias, q_bf, kt_bf, v_bf)


def _attn_decode_kernel(pt_ref, bias_ref, q_ref, ktn_ref, vtn_ref, *rest, n_heads, dec_seq, ppb):
    k_refs = rest[:ppb]
    v_refs = rest[ppb:2 * ppb]
    o_ref = rest[2 * ppb]
    acc_ref, r_ref, qbd_ref = rest[2 * ppb + 1:]
    j = pl.program_id(1)
    rows = n_heads * dec_seq
    w_all = n_heads * HEAD_DIM
    slot = lax.broadcasted_iota(jnp.int32, (rows, LANES), 1)
    rid = lax.broadcasted_iota(jnp.int32, (rows, LANES), 0)
    trow = lax.broadcasted_iota(jnp.int32, (LANES, LANES), 0)
    tcol = lax.broadcasted_iota(jnp.int32, (LANES, LANES), 1)
    tri = (trow > tcol).astype(BF16)
    bias = bias_ref[...] * LOG2E

    def pages(kts, vts, mask, acc, later):
        qbd = qbd_ref[...]
        zs = [jnp.dot(qbd, kt.astype(BF16), preferred_element_type=F32) * LOG2E + bias for kt in kts]
        sps = [_sb_softplus(z, mask) for z in zs]
        loc_all = jnp.dot(jnp.concatenate(sps, axis=0).astype(BF16), tri, preferred_element_type=F32)
        for m, (z, sp, vt) in enumerate(zip(zs, sps, vts)):
            w = _sb_weights(z, sp, loc_all[m * rows:(m + 1) * rows] + later, mask)
            acc = acc + lax.dot_general(w.astype(BF16), vt.astype(BF16), (((1,), (1,)), ((), ())),
                                        preferred_element_type=F32)
            later = later + jnp.sum(sp, axis=1, keepdims=True)
        return acc, later

    @pl.when(j == 0)
    def _():
        q = q_ref[...]
        lq = lax.broadcasted_iota(jnp.int32, q.shape, 1)
        blocks = [jnp.where((lq >= h * HEAD_DIM) & (lq < (h + 1) * HEAD_DIM), q, 0.0) for h in range(n_heads)]
        qbd_ref[...] = jnp.concatenate(blocks, axis=0).astype(BF16)
        acc, later = pages([ktn_ref[...]], [vtn_ref[...]], slot < (rid & (dec_seq - 1)),
                           jnp.zeros(acc_ref.shape, F32), jnp.zeros(r_ref.shape, F32))
        acc_ref[...] = acc
        r_ref[...] = later

    acc, later = pages([k_refs[m][...].reshape(w_all, LANES) for m in range(ppb)],
                       [v_refs[m][...].reshape(w_all, LANES) for m in range(ppb)], None, acc_ref[...], r_ref[...])
    acc_ref[...] = acc
    r_ref[...] = later

    @pl.when(j == pl.num_programs(1) - 1)
    def _():
        acc = acc_ref[...]
        la = lax.broadcasted_iota(jnp.int32, (dec_seq, w_all), 1)
        out = jnp.zeros((dec_seq, w_all), F32)
        for h in range(n_heads):
            sel = (la >= h * HEAD_DIM) & (la < (h + 1) * HEAD_DIM)
            out = jnp.where(sel, acc[h * dec_seq:(h + 1) * dec_seq, :], out)
        o_ref[...] = out


def _pages_per_step(n_pages):
    return next(c for c in (32, 16, 8, 4, 2, 1) if n_pages % c == 0)


def _attn_decode(q, ktn, vtn, ck_t, cv_t, layer, page_table, bias_rows, ppb):
    bs, t, w_all = q.shape
    n_heads = w_all // HEAD_DIM
    n_pages = page_table.shape[1]
    steps = n_pages // ppb
    rows = n_heads * t

    def page_spec(m):
        return pl.BlockSpec((None, None, n_heads, HEAD_DIM, LANES),
                            lambda b, j, pt, m=m: (layer, pt[b, n_pages - 1 - (j * ppb + m)], 0, 0, 0))

    grid_spec = pltpu.PrefetchScalarGridSpec(
        num_scalar_prefetch=1,
        grid=(bs, steps),
        in_specs=[
            pl.BlockSpec((rows, 1), lambda b, j, pt: (0, 0)),
            pl.BlockSpec((None, t, w_all), lambda b, j, pt: (b, 0, 0)),
            pl.BlockSpec((None, w_all, LANES), lambda b, j, pt: (b, 0, 0)),
            pl.BlockSpec((None, w_all, LANES), lambda b, j, pt: (b, 0, 0)),
        ] + [page_spec(m) for m in range(ppb)] + [page_spec(m) for m in range(ppb)],
        out_specs=pl.BlockSpec((None, t, w_all), lambda b, j, pt: (b, 0, 0)),
        scratch_shapes=[pltpu.VMEM((rows, w_all), F32), pltpu.VMEM((rows, 1), F32), pltpu.VMEM((rows, w_all), BF16)],
    )
    return pl.pallas_call(
        functools.partial(_attn_decode_kernel, n_heads=n_heads, dec_seq=t, ppb=ppb),
        grid_spec=grid_spec,
        out_shape=jax.ShapeDtypeStruct((bs, t, w_all), F32),
        compiler_params=_cparams("parallel", "arbitrary"),
        name="sb_attn_decode",
    )(page_table, bias_rows, q, ktn, vtn, *([ck_t] * ppb), *([cv_t] * ppb))


def _s5_disc_kernel(are_ref, aim_ref, ldt_ref, bre_ref, bim_ref, abr_ref, abi_ref, bbr_ref, bbi_ref):
    dt = jnp.exp(ldt_ref[...])
    lam_re = jnp.minimum(are_ref[...], -1e-4)
    lam_im = aim_ref[...]
    mag = jnp.exp(lam_re * dt)
    ab_re = mag * jnp.cos(lam_im * dt)
    ab_im = mag * jnp.sin(lam_im * dt)
    den = lam_re * lam_re + lam_im * lam_im
    nr = ab_re - 1.0
    f_re = (nr * lam_re + ab_im * lam_im) / den
    f_im = (ab_im * lam_re - nr * lam_im) / den
    abr_ref[...] = ab_re
    abi_ref[...] = ab_im
    br = bre_ref[...]
    bi = bim_ref[...]
    bbr_ref[...] = f_re * br - f_im * bi
    bbi_ref[...] = f_re * bi + f_im * br


def _s5_disc(a_re, a_im, log_dt, b_re_t, b_im_t):
    g, _, p = a_re.shape
    ch = b_re_t.shape[1]
    return pl.pallas_call(
        _s5_disc_kernel,
        out_shape=[jax.ShapeDtypeStruct((g, 1, p), F32)] * 2 + [jax.ShapeDtypeStruct((g, ch, p), F32)] * 2,
        name="s5_disc",
    )(a_re, a_im, log_dt, b_re_t, b_im_t)


def _cmul(ar, ai, xr, xi):
    return ar * xr - ai * xi, ar * xi + ai * xr


def _split3(x):
    hi = x.astype(BF16)
    r1 = x - hi.astype(F32)
    mid = r1.astype(BF16)
    return hi, mid, (r1 - mid.astype(F32)).astype(BF16)


def _s5_kernel(u_ref, perm_ref, permt_ref, x0r_ref, x0i_ref, a1r_ref, a1i_ref, wb_ref, wc_ref, d_ref, wg_ref,
               y_ref, fr_ref, fi_ref, st_r, st_i, cr_ref, ci_ref, xs_ref, *, nb, tp):
    step = pl.program_id(0)
    tr = nb * tp
    ts, c = u_ref.shape[1:]
    rows = nb * ts
    ns = a1r_ref.shape[1]

    @pl.when(step == 0)
    def _():
        pad = jnp.zeros((tr - nb, ns), F32)
        st_r[...] = jnp.concatenate([pad, x0r_ref[...]], axis=0) if tp > 1 else x0r_ref[...]
        st_i[...] = jnp.concatenate([pad, x0i_ref[...]], axis=0) if tp > 1 else x0i_ref[...]

    u_bt = u_ref[...].reshape(rows, c)
    u_tm = jnp.dot(perm_ref[...], u_bt.astype(BF16), preferred_element_type=F32)
    bu = _bdot(u_tm, wb_ref[...])
    bur, bui = bu[:, :ns], bu[:, ns:]
    a1r, a1i = a1r_ref[...], a1i_ref[...]
    rid = lax.broadcasted_iota(jnp.int32, (rows, ns), 0)
    if tp > 1:
        sr = jnp.where(rid < nb, 0.0, pltpu.roll(bur, nb, axis=0))
        si = jnp.where(rid < nb, 0.0, pltpu.roll(bui, nb, axis=0))
        pr, pi = _cmul(a1r, a1i, sr, si)
        cr = bur + pr
        ci = bui + pi
    else:
        cr, ci = bur, bui
    cr_ref[...] = cr
    ci_ref[...] = ci
    tid = lax.broadcasted_iota(jnp.int32, (tr, ns), 0)
    lr, li = st_r[...], st_i[...]
    if tp > 1:
        lr = jnp.where(tid < nb, pltpu.roll(lr, nb, axis=0), 0.0)
        li = jnp.where(tid < nb, pltpu.roll(li, nb, axis=0), 0.0)
    fr, fi = _cmul(a1r, a1i, lr, li)
    if tp > 1:
        gr, gi = _cmul(a1r, a1i, pltpu.roll(fr, nb, axis=0), pltpu.roll(fi, nb, axis=0))
        fr = jnp.where(tid < nb, fr, gr)
        fi = jnp.where(tid < nb, fi, gi)
    cr_ref[0:tr, :] = cr_ref[0:tr, :] + fr
    ci_ref[0:tr, :] = ci_ref[0:tr, :] + fi

    apr, api = _cmul(a1r, a1i, a1r, a1i) if tp > 1 else (a1r, a1i)
    apr = jnp.broadcast_to(apr, (tr, ns))
    api = jnp.broadcast_to(api, (tr, ns))

    def body(k, carry):
        xr, xi = carry
        off = pl.multiple_of(k * tr, tr)
        nr, ni = _cmul(apr, api, xr, xi)
        nr = nr + cr_ref[pl.ds(off, tr), :]
        ni = ni + ci_ref[pl.ds(off, tr), :]
        xs_ref[pl.ds(off, tr), 0:ns] = nr
        xs_ref[pl.ds(off, tr), ns:2 * ns] = ni
        return nr, ni

    xr, xi = lax.fori_loop(0, rows // tr, body, (jnp.zeros((tr, ns), F32), jnp.zeros((tr, ns), F32)))
    st_r[...] = xr
    st_i[...] = xi
    fr_ref[...] = xr
    fi_ref[...] = xi

    y_tm = _bdot(xs_ref[...], wc_ref[...])
    permt = permt_ref[...]
    y = sum(jnp.dot(permt, part, preferred_element_type=F32) for part in _split3(y_tm)) + d_ref[...] * u_bt
    g = 0.5 * y * (1.0 + jnp.tanh(math.sqrt(2.0 / math.pi) * (y + 0.044715 * (y * y * y))))
    y_ref[...] = (g * _sigmoid(_bdot(g, wg_ref[...]))).reshape(nb, ts, c)


def _s5(u3, x0r, x0i, a1r, a1i, wb, wc, d_skip, w_glu, ts):
    nb, s, c = u3.shape
    ns = a1r.shape[1]
    tp = max(1, SUBLANES // nb)
    tr = nb * tp
    rows = ts * nb
    dest = jnp.arange(rows)
    src = (dest % nb) * ts + dest // nb
    perm = (src[:, None] == dest[None, :]).astype(BF16)
    blk = pl.BlockSpec((nb, ts, c), lambda i: (0, i, 0))
    return pl.pallas_call(
        functools.partial(_s5_kernel, nb=nb, tp=tp),
        grid=(s // ts,),
        in_specs=[
            blk, _full((rows, rows)), _full((rows, rows)),
            _full((nb, ns)), _full((nb, ns)),
            _full((1, ns)), _full((1, ns)),
            _full(wb.shape), _full(wc.shape), _full((1, c)), _full(w_glu.shape),
        ],
        out_specs=[blk, _full((tr, ns)), _full((tr, ns))],
        out_shape=[jax.ShapeDtypeStruct((nb, s, c), F32), jax.ShapeDtypeStruct((tr, ns), F32),
                   jax.ShapeDtypeStruct((tr, ns), F32)],
        scratch_shapes=[pltpu.VMEM((tr, ns), F32), pltpu.VMEM((tr, ns), F32),
                        pltpu.VMEM((rows, ns), F32), pltpu.VMEM((rows, ns), F32),
                        pltpu.VMEM((rows, 2 * ns), F32)],
        compiler_params=_cparams("arbitrary"),
        name="s5_scan",
    )(u3, perm, perm.T, x0r, x0i, a1r, a1i, wb, wc, d_skip, w_glu)


def _gdn_kernel(*refs, n_heads, ch, n_seq, carry_conv):
    if carry_conv:
        (xc_ref, halo_ref, z_ref, ab_ref, wconv_ref, alog_ref, dtb_ref, gn_ref, hm_ref, ex_ref, s0_ref,
         oc_ref, sfin_ref, s_scr, tail_scr, x_scr, ac_scr, qd_scr, kd_scr, egl_scr, o_scr, m_scr, t_scr) = refs
    else:
        (xc_ref, sh1_ref, sh2_ref, sh3_ref, z_ref, ab_ref, wconv_ref, alog_ref, dtb_ref, gn_ref, hm_ref, ex_ref,
         s0_ref, oc_ref, sfin_ref, s_scr, x_scr, ac_scr, qd_scr, kd_scr, egl_scr, o_scr, m_scr, t_scr) = refs
    step = pl.program_id(1)
    r = xc_ref.shape[0]
    w_all = n_heads * HEAD_DIM
    nc = r // (n_seq * ch)

    @pl.when(step == 0)
    def _():
        s_scr[...] = s0_ref[...]
        if carry_conv:
            tail_scr[...] = halo_ref[...]

    x = xc_ref[...]
    if carry_conv:
        ext = jnp.concatenate([tail_scr[...], x], axis=0)
        sh = [ext[SUBLANES - j:SUBLANES - j + r] for j in (1, 2, 3)]
        tail_scr[...] = x[r - SUBLANES:, :]
    else:
        sh = [sh1_ref[...], sh2_ref[...], sh3_ref[...]]
    wc = wconv_ref[...]
    conv = sh[2] * wc[0:1] + sh[1] * wc[1:2] + sh[0] * wc[2:3] + x * wc[3:4]
    conv = conv * _sigmoid(conv)
    q = conv[:, 0:w_all]
    k = conv[:, w_all:2 * w_all]
    v = conv[:, 2 * w_all:3 * w_all]
    hm = hm_ref[...]
    q = q * lax.rsqrt(_bdot(q * q, hm) + EPS) * (HEAD_DIM ** -0.5)
    k = k * lax.rsqrt(_bdot(k * k, hm) + EPS)

    ab = ab_ref[...]
    g = -jnp.exp(alog_ref[...]) * _softplus(ab + dtb_ref[...])
    beta = _sigmoid(ab)
    ri = lax.broadcasted_iota(jnp.int32, (r, r), 0)
    ci = lax.broadcasted_iota(jnp.int32, (r, r), 1)
    lg = int(math.log2(ch))
    same = (ri >> lg) == (ci >> lg)
    lower = same & (ci <= ri)
    strict = same & (ci < ri)
    g_cum = _dot2_left(lower.astype(BF16), g)
    g_tot = _dot2_left(same.astype(BF16), g)
    ex = ex_ref[...]
    gc_e = _dot2(g_cum, ex)
    gl_e = _dot2(g_tot, ex)
    beta_e = _dot2(pltpu.roll(beta, LANES - n_heads, axis=1), ex)
    g_t = g_cum.T

    e_gc = jnp.exp(gc_e)
    kb = k * beta_e
    xall = jnp.concatenate([v * beta_e, kb * e_gc], axis=1)
    qd_scr[...] = q * e_gc
    kd_scr[...] = k * jnp.exp(gl_e - gc_e)
    egl_scr[...] = jnp.exp(gl_e)
    selm = ((lax.broadcasted_iota(jnp.int32, (r, LANES), 0) & (ch - 1))
            == lax.broadcasted_iota(jnp.int32, (r, LANES), 1)).astype(BF16)

    def level_mask(ls):
        return (((ri >> (ls + 1)) == (ci >> (ls + 1))) & (((ri >> ls) & 1) == 1) & (((ci >> ls) & 1) == 0))

    eye = (ri == ci).astype(F32)
    heads = range(n_heads)
    hsl = [slice(h * HEAD_DIM, (h + 1) * HEAD_DIM) for h in heads]
    nt = (((1,), (1,)), ((), ()))
    khs = [k[:, hs].astype(BF16) for hs in hsl]
    kks = [lax.dot_general(kb[:, hs].astype(BF16), kh, nt, preferred_element_type=F32) for hs, kh in zip(hsl, khs)]
    qks = [lax.dot_general(q[:, hs].astype(BF16), kh, nt, preferred_element_type=F32) for hs, kh in zip(hsl, khs)]
    mask0 = level_mask(0)
    attns = []
    for h in heads:
        diff = g_cum[:, h:h + 1] - g_t[h:h + 1, :]
        decay = jnp.exp(jnp.where(lower, diff, 0.0))
        m = jnp.where(strict, kks[h] * decay, 0.0)
        m_scr[h] = m
        t_scr[h] = eye - jnp.where(mask0, m, 0.0)
        attns.append(jnp.where(lower, qks[h] * decay, 0.0).astype(BF16))
    for h in heads:
        ac_scr[h] = jnp.dot(attns[h], selm, preferred_element_type=F32)
    for ls in range(1, lg):
        mask = level_mask(ls)
        tbs = [t_scr[h].astype(BF16) for h in heads]
        ys = [jnp.dot(tbs[h], jnp.where(mask, m_scr[h], 0.0).astype(BF16), preferred_element_type=F32)
              for h in heads]
        for h in heads:
            t_scr[h] = t_scr[h] - jnp.dot(ys[h].astype(BF16), tbs[h], preferred_element_type=F32)
    for h in heads:
        xh = jnp.concatenate([xall[:, hsl[h]], xall[:, w_all + h * HEAD_DIM:w_all + (h + 1) * HEAD_DIM]], axis=1)
        x_scr[h] = _dot2(t_scr[h], xh.astype(BF16))

    def seq_body(s, _):
        sts = [s_scr[s, h] for h in heads]
        for c in range(nc):
            off = pl.multiple_of((s * nc + c) * ch, ch)
            rows = pl.ds(off, ch)
            for h in heads:
                hs = hsl[h]
                xh = x_scr[h, rows, :]
                stb = sts[h].astype(BF16)
                v_new = xh[:, 0:HEAD_DIM] - jnp.dot(xh[:, HEAD_DIM:].astype(BF16), stb, preferred_element_type=F32)
                vnb = v_new.astype(BF16)
                o = jnp.dot(qd_scr[rows, hs].astype(BF16), stb, preferred_element_type=F32)
                o = o + jnp.dot(ac_scr[h, rows, 0:ch].astype(BF16), vnb, preferred_element_type=F32)
                o_scr[rows, hs] = o
                sts[h] = sts[h] * egl_scr[pl.ds(off, 1), hs] + lax.dot_general(
                    kd_scr[rows, hs].astype(BF16), vnb, (((0,), (0,)), ((), ())), preferred_element_type=F32)
        for h in heads:
            s_scr[s, h] = sts[h]
        return 0

    lax.fori_loop(0, n_seq, seq_body, 0)

    o = o_scr[...]
    o = o * lax.rsqrt(_bdot(o * o, hm) * (1.0 / HEAD_DIM) + EPS) * gn_ref[...]
    zz = z_ref[...]
    oc_ref[...] = o * (zz * _sigmoid(zz))

    @pl.when(step == pl.num_programs(1) - 1)
    def _():
        sfin_ref[...] = s_scr[...]


def _gdn(xc, shifts, halo, z, ab, wconv, alog, dtb, gn, hm, ex, s0, n_batch, rb, ch, n_seq):
    n, w3 = xc.shape
    w_all = w3 // 3
    n_heads = w_all // HEAD_DIM
    per = n // n_batch // rb
    carry_conv = shifts is None
    row_spec = lambda w: pl.BlockSpec((rb, w), lambda b, i: (b * per + i, 0))
    st_spec = pl.BlockSpec((n_seq, n_heads, HEAD_DIM, HEAD_DIM), lambda b, i: (b, 0, 0, 0))
    ins = [xc]
    specs = [row_spec(w3)]
    if carry_conv:
        ins.append(halo)
        specs.append(pl.BlockSpec((None, SUBLANES, w3), lambda b, i: (b, 0, 0)))
    else:
        ins += list(shifts)
        specs += [row_spec(w3)] * 3
    ins += [z, ab, wconv, alog, dtb, gn, hm, ex, s0]
    specs += [row_spec(w_all), row_spec(LANES), _full(wconv.shape), _full((1, LANES)), _full((1, LANES)),
              _full((1, w_all)), _full(hm.shape), _full(ex.shape), st_spec]
    scratch = [pltpu.VMEM((n_seq, n_heads, HEAD_DIM, HEAD_DIM), F32)]
    if carry_conv:
        scratch.append(pltpu.VMEM((SUBLANES, w3), F32))
    scratch += [pltpu.VMEM((n_heads, rb, 2 * HEAD_DIM), F32), pltpu.VMEM((n_heads, rb, LANES), F32),
                pltpu.VMEM((rb, w_all), F32), pltpu.VMEM((rb, w_all), F32), pltpu.VMEM((rb, w_all), F32),
                pltpu.VMEM((rb, w_all), F32), pltpu.VMEM((n_heads, rb, rb), F32), pltpu.VMEM((n_heads, rb, rb), F32)]
    return pl.pallas_call(
        functools.partial(_gdn_kernel, n_heads=n_heads, ch=ch, n_seq=n_seq, carry_conv=carry_conv),
        grid=(n_batch, per),
        in_specs=specs,
        out_specs=[row_spec(w_all), st_spec],
        out_shape=[jax.ShapeDtypeStruct((n, w_all), F32),
                   jax.ShapeDtypeStruct((n_batch * n_seq, n_heads, HEAD_DIM, HEAD_DIM), F32)],
        scratch_shapes=scratch,
        compiler_params=_cparams("parallel", "arbitrary"),
        name="gdn",
    )(*ins)


def _out_ffn_kernel(x_ref, oa_ref, ob_ref, oc_ref, g1_ref, sc_ref, sh_ref, g2_ref, gain_ref,
                    wo_ref, wg_ref, wu_ref, wd_ref, y_ref, *, wa, wb, n_f):
    mix = _bdot(oa_ref[...], wo_ref[0:wa, :])
    mix = mix + _bdot(ob_ref[...], wo_ref[wa:wa + wb, :])
    mix = mix + _bdot(oc_ref[...], wo_ref[wa + wb:, :])
    x = x_ref[...] + g1_ref[...] * mix
    h = x * lax.rsqrt(jnp.mean(x * x, axis=-1, keepdims=True) + EPS) * gain_ref[...]
    h = h * (1.0 + sc_ref[...]) + sh_ref[...]
    hb = h.astype(BF16)
    f = wg_ref.shape[1]
    fc = f // n_f
    ff = jnp.zeros(x.shape, F32)
    for c in range(n_f):
        gt = jnp.dot(hb, wg_ref[:, c * fc:(c + 1) * fc], preferred_element_type=F32)
        up = jnp.dot(hb, wu_ref[:, c * fc:(c + 1) * fc], preferred_element_type=F32)
        act = (gt * _sigmoid(gt)) * up
        ff = ff + jnp.dot(act.astype(BF16), wd_ref[c * fc:(c + 1) * fc, :], preferred_element_type=F32)
    y_ref[...] = x + g2_ref[...] * ff


def _out_ffn(x, oa, ob, oc, g1, sc2, sh2, g2, gain, wo, wg, wu, wd, tb):
    n, d = x.shape
    nb = n // tb
    per = nb // g1.shape[0]
    r = g1.shape[1]
    f = wg.shape[1]
    n_f = 2 if (f // 2) % LANES == 0 else 1
    mod_spec = pl.BlockSpec((None, r, d), lambda i: (i // per, 0, 0))
    row = lambda w: pl.BlockSpec((tb, w), lambda i: (i, 0))
    return pl.pallas_call(
        functools.partial(_out_ffn_kernel, wa=oa.shape[1], wb=ob.shape[1], n_f=n_f),
        grid=(nb,),
        in_specs=[row(d), row(oa.shape[1]), row(ob.shape[1]), row(oc.shape[1]),
                  mod_spec, mod_spec, mod_spec, mod_spec, _full((1, d)),
                  _resident(wo.shape), _resident(wg.shape), _resident(wu.shape), _resident(wd.shape)],
        out_specs=row(d),
        out_shape=jax.ShapeDtypeStruct((n, d), F32),
        compiler_params=_cparams("parallel"),
        name="out_ffn",
    )(x, oa, ob, oc, g1, sc2, sh2, g2, gain, wo, wg, wu, wd)


def _block_diag(blocks):
    g, a, b = blocks.shape
    eye = jnp.eye(g, dtype=blocks.dtype)
    return (blocks[:, :, None, :] * eye[:, None, :, None]).reshape(g * a, g * b)


def _mods(mod, rows_per_seq, tb):
    d = mod.shape[1] // N_ADA
    parts = [mod[:, i * d:(i + 1) * d] for i in range(N_ADA)]
    if rows_per_seq >= tb:
        return [p[:, None, :] for p in parts]
    rep = [jnp.repeat(p, rows_per_seq, axis=0) for p in parts]
    return [p.reshape(-1, tb, d) for p in rep]


def kernel(x_prompt, x_sample, cache_k, cache_v, state_conv, state_ssm_re, state_ssm_im, state_gdn, page_table, c_prompt, c_sample, w_ada, b_ada, norm_mix, norm_ffn, w_in, sb_qnorm, sb_knorm, sb_bias, ssm_a_re, ssm_a_im, ssm_log_dt, ssm_b_re, ssm_b_im, ssm_c_re, ssm_c_im, ssm_d, ssm_w_glu, gdn_conv, gdn_a_log, gdn_dt_bias, gdn_norm, w_out, ffn_gate, ffn_up, ffn_down):
    bp, seq, d = x_prompt.shape
    bs, dseq, _ = x_sample.shape
    depth = w_ada.shape[0]
    n_sb = sb_bias.shape[1]
    n_gd = gdn_a_log.shape[1]
    n_grp, n_state = ssm_a_re.shape[1:]
    w_sb = n_sb * HEAD_DIM
    w_gd = n_gd * HEAD_DIM
    w_ssm = n_grp * SSM_CH
    sizes = (w_sb, w_sb, w_sb, w_ssm, 3 * w_gd, w_gd)
    offs = [0]
    for s_ in sizes:
        offs.append(offs[-1] + s_)
    offs.append(offs[-1] + LANES)
    offs = tuple(offs)
    in_width = w_in.shape[2]
    ns = n_grp * n_state
    n_p = bp * seq
    n_s = bs * dseq
    page = cache_k.shape[2]

    tb_p = min(512, seq)
    tb_f = min(512, seq)
    tq = min(512, seq)
    tk = min(256, seq)
    rb_p = min(256, seq)
    ch_p = math.gcd(seq, GDN_CHUNK)
    ch_s = math.gcd(dseq, GDN_CHUNK)
    ts_p = min(128, seq)

    w_in_bf = jnp.pad(w_in, ((0, 0), (0, 0), (0, offs[-1] - in_width))).astype(BF16)
    w_out_bf = w_out.astype(BF16)
    wg_bf, wu_bf, wd_bf = ffn_gate.astype(BF16), ffn_up.astype(BF16), ffn_down.astype(BF16)
    hm_sb = _block_diag(jnp.ones((n_sb, HEAD_DIM, HEAD_DIM), BF16))
    hm_gd = _block_diag(jnp.ones((n_gd, HEAD_DIM, HEAD_DIM), BF16))
    ex_gd = jnp.repeat(jnp.eye(LANES, n_gd, dtype=BF16), HEAD_DIM, axis=1)
    qg = jnp.tile(sb_qnorm, (1, n_sb))[:, None, :]
    kg = jnp.tile(sb_knorm, (1, n_sb))[:, None, :]
    gng = jnp.tile(gdn_norm, (1, n_gd))[:, None, :]
    alog = jnp.pad(gdn_a_log, ((0, 0), (0, LANES - n_gd)))[:, None, :]
    dtb = jnp.pad(gdn_dt_bias, ((0, 0), (0, LANES - n_gd)))[:, None, :]
    bias_rows = jnp.repeat(sb_bias, dseq, axis=1)[:, :, None]
    ck_t = jnp.transpose(cache_k, (0, 1, 3, 4, 2))
    cv_t = jnp.transpose(cache_v, (0, 1, 3, 4, 2))

    mod_all = _ada(jnp.concatenate([c_prompt, c_sample], axis=0), w_ada, b_ada)

    xp = x_prompt.reshape(n_p, d)
    xs = x_sample.reshape(n_s, d)
    st_p = [[] for _ in range(6)]
    st_s = [[] for _ in range(6)]
    zero_halo = jnp.zeros((bp, SUBLANES, 3 * w_gd), F32)
    zero_ssm = jnp.zeros((bp, ns), F32)
    zero_gdn = jnp.zeros((bp, n_gd, HEAD_DIM, HEAD_DIM), F32)
    tp_p = max(1, SUBLANES // bp)
    tp_s = max(1, SUBLANES // bs)

    for l in range(depth):
        abr, abi, bbr, bbi = _s5_disc(ssm_a_re[l][:, None, :], ssm_a_im[l][:, None, :], ssm_log_dt[l][:, None, None],
                                      jnp.transpose(ssm_b_re[l], (0, 2, 1)), jnp.transpose(ssm_b_im[l], (0, 2, 1)))
        a1r, a1i = abr.reshape(1, ns), abi.reshape(1, ns)
        wb = jnp.concatenate([_block_diag(bbr), _block_diag(bbi)], axis=1).astype(BF16)
        wc = jnp.concatenate([_block_diag(jnp.transpose(ssm_c_re[l], (0, 2, 1))),
                              -_block_diag(jnp.transpose(ssm_c_im[l], (0, 2, 1)))], axis=0).astype(BF16)
        d_skip = ssm_d[l][None, :]
        wglu_bf = ssm_w_glu[l].astype(BF16)

        mods_p = _mods(mod_all[l, :bp], seq, tb_p)
        mods_pf = _mods(mod_all[l, :bp], seq, tb_f)
        mods_s = _mods(mod_all[l, bp:], dseq, n_s)

        for grp in ("p", "s"):
            if grp == "p":
                x, mods, modsf, tb, tbf = xp, mods_p, mods_pf, tb_p, tb_f
            else:
                x, mods, modsf, tb, tbf = xs, mods_s, mods_s, n_s, n_s
            sh1, sc1, g1, sh2, sc2, g2 = mods
            q_bf, k, v, u, xc, z, ab, *attn_in = _inproj(x, sc1, sh1, norm_mix[l][None, :], w_in_bf[l], qg[l], kg[l],
                                                         hm_sb, offs, tb, seq if grp == "p" else None)
            if grp == "p":
                kt_bf, v_bf = attn_in
                oa = _attn_prompt(q_bf.reshape(bp, seq, w_sb), kt_bf, v_bf.reshape(bp, seq, w_sb), sb_bias[l],
                                  tq, tk).reshape(n_p, w_sb)
                nb, steps, x0r, x0i = bp, seq, zero_ssm, zero_ssm
                ts, tp = ts_p, tp_p
            else:
                pad = ((0, 0), (0, 0), (0, page - dseq))
                ktn = jnp.pad(jnp.swapaxes(k.reshape(bs, dseq, w_sb), 1, 2), pad)
                vtn = jnp.pad(jnp.swapaxes(v.reshape(bs, dseq, w_sb), 1, 2), pad)
                oa = _attn_decode(q_bf.astype(F32).reshape(bs, dseq, w_sb), ktn, vtn, ck_t, cv_t, l, page_table,
                                  bias_rows[l], _pages_per_step(page_table.shape[1])).reshape(n_s, w_sb)
                nb, steps = bs, dseq
                x0r, x0i = state_ssm_re[l].reshape(bs, ns), state_ssm_im[l].reshape(bs, ns)
                ts, tp = dseq, tp_s

            ob3, fr, fi = _s5(u.reshape(nb, steps, w_ssm), x0r, x0i, a1r, a1i, wb, wc, d_skip, wglu_bf, ts)
            ob = ob3.reshape(steps * nb, w_ssm)
            ssm_re = fr[-nb:].reshape(nb, n_grp, n_state)
            ssm_im = fi[-nb:].reshape(nb, n_grp, n_state)

            if grp == "p":
                oc, gdn_s = _gdn(xc, None, zero_halo, z, ab, gdn_conv[l], alog[l], dtb[l], gng[l], hm_gd, ex_gd,
                                 zero_gdn, bp, rb_p, ch_p, 1)
                new_buf = xc.reshape(bp, seq, 3 * w_gd)[:, seq - (CONV_WIDTH - 1):]
            else:
                ext = jnp.concatenate([state_conv[l], xc.reshape(bs, dseq, 3 * w_gd)], axis=1)
                shifts = [ext[:, CONV_WIDTH - 1 - j:CONV_WIDTH - 1 - j + dseq].reshape(n_s, 3 * w_gd)
                          for j in (1, 2, 3)]
                oc, gdn_s = _gdn(xc, shifts, None, z, ab, gdn_conv[l], alog[l], dtb[l], gng[l], hm_gd, ex_gd,
                                 state_gdn[l], 1, n_s, ch_s, bs)
                new_buf = ext[:, dseq:]

            x = _out_ffn(x, oa, ob, oc, g1, sc2, sh2, g2, norm_ffn[l][None, :], w_out_bf[l], wg_bf[l], wu_bf[l],
                         wd_bf[l], tbf) if grp == "s" else _out_ffn(
                x, oa, ob, oc, modsf[2], modsf[4], modsf[3], modsf[5], norm_ffn[l][None, :], w_out_bf[l], wg_bf[l],
                wu_bf[l], wd_bf[l], tbf)
            bn = bp if grp == "p" else bs
            sq = seq if grp == "p" else dseq
            if grp == "p":
                k_st = jnp.transpose(k.reshape(bn, n_sb, HEAD_DIM, sq), (0, 3, 1, 2))
                v_st = jnp.transpose(v.reshape(bn, n_sb, HEAD_DIM, sq), (0, 3, 1, 2))
            else:
                k_st, v_st = k.reshape(bn, sq, n_sb, HEAD_DIM), v.reshape(bn, sq, n_sb, HEAD_DIM)
            new = (k_st, v_st, new_buf, ssm_re, ssm_im, gdn_s)
            tgt = st_p if grp == "p" else st_s
            for i_ in range(6):
                tgt[i_].append(new[i_])
            if grp == "p":
                xp = x
            else:
                xs = x

    outs_p = [jnp.stack(t) for t in st_p]
    outs_s = [jnp.stack(t) for t in st_s]
    return (xp.reshape(bp, seq, d), xs.reshape(bs, dseq, d), *outs_p, *outs_s)
```
